```python
import jax, jax.numpy as jnp
from jax import lax
import numpy as np

D_MODEL = 2048
BATCH = 2
SEQ = 8192
DEPTH = 1

GRID_W = 64
CTX_LEN = 256
N_HEADS_NA = 16
HEAD_DIM = 64
D_NA = N_HEADS_NA * HEAD_DIM
NA_KH = 8
NA_KW = 16
NA_QB = 16
NA_REG = 2 * NA_KW
FOURIER_GROUPS = 8
FOURIER_GROUP_DIM = 128
D_FOURIER = FOURIER_GROUPS * FOURIER_GROUP_DIM
D_PROJ = 3 * D_NA + D_FOURIER + 2 * D_MODEL
N_MOD = 6
N_EXPERTS = 256
TOP_K = 8
N_GROUPS = 8
TOPK_GROUPS = 4
D_EXPERT = 512
ROUTED_SCALE = 2.5
EXPERT_BLOCK = 128
EPS = 1e-6

kernel_name = "hybrid_na_fourier_moe_prefix_dit"


def _rmsnorm(x, g):
    x32 = x.astype(jnp.float32)
    y = x32 * lax.rsqrt(jnp.mean(x32 * x32, axis=-1, keepdims=True) + EPS)
    return y.astype(x.dtype) * g


def _modulate(h, shift, scale):
    return h * (1 + scale) + shift


def _split_proj(p):
    cuts = [D_NA, 2 * D_NA, 3 * D_NA, 3 * D_NA + D_FOURIER, 3 * D_NA + D_FOURIER + D_MODEL]
    return jnp.split(p, cuts, axis=-1)


def _heads(t):
    return t.reshape(t.shape[0], t.shape[1], N_HEADS_NA, HEAD_DIM)


def _na_latent(q, k, v, k_ctx, v_ctx, rpb):
    B, S, H, Dh = q.shape
    rows = S // GRID_W
    kh = min(NA_KH, rows)
    nqb = GRID_W // NA_QB
    scale = Dh ** -0.5
    qcol = np.arange(GRID_W).reshape(nqb, NA_QB)
    reg_start = np.clip(np.arange(nqb) * NA_QB - NA_KW // 2, 0, GRID_W - NA_REG)
    reg_cols = reg_start[:, None] + np.arange(NA_REG)
    win_start = np.clip(qcol - NA_KW // 2, 0, GRID_W - NA_KW)
    rc = reg_cols[:, None, :]
    in_win = (rc >= win_start[..., None]) & (rc < win_start[..., None] + NA_KW)
    col_idx = np.clip(rc - qcol[..., None], -(NA_KW - 1), NA_KW - 1) + NA_KW - 1

    qg = q.reshape(B, rows, nqb, NA_QB, H, Dh)
    kg = k.reshape(B, rows, GRID_W, H, Dh)
    vg = v.reshape(B, rows, GRID_W, H, Dh)

    def one_row(r):
        rs = jnp.clip(r - kh // 2, 0, rows - kh)
        k_reg = lax.dynamic_slice_in_dim(kg, rs, kh, axis=1)[:, :, reg_cols]
        v_reg = lax.dynamic_slice_in_dim(vg, rs, kh, axis=1)[:, :, reg_cols]
        q_r = lax.dynamic_index_in_dim(qg, r, axis=1, keepdims=False)
        s_lat = jnp.einsum('bjqhd,bijkhd->bhjqik', q_r, k_reg).astype(jnp.float32) * scale
        row_idx = rs + jnp.arange(kh) - r + NA_KH - 1
        bias = rpb[:, row_idx[None, None, :, None], col_idx[:, :, None, :]]
        s_lat = s_lat + bias[None].astype(jnp.float32)
        s_lat = jnp.where(in_win[:, :, None, :], s_lat, -1e30)
        s_lat = s_lat.reshape(B, H, nqb, NA_QB, kh * NA_REG)
        s_ctx = jnp.einsum('bjqhd,bkhd->bhjqk', q_r, k_ctx).astype(jnp.float32) * scale
        p = jax.nn.softmax(jnp.concatenate([s_lat, s_ctx], axis=-1), axis=-1).astype(v.dtype)
        p_lat = p[..., :kh * NA_REG].reshape(B, H, nqb, NA_QB, kh, NA_REG)
        p_ctx = p[..., kh * NA_REG:]
        return (jnp.einsum('bhjqik,bijkhd->bjqhd', p_lat, v_reg)
                + jnp.einsum('bhjqk,bkhd->bjqhd', p_ctx, v_ctx))

    out = lax.map(one_row, jnp.arange(rows))
    return jnp.moveaxis(out, 0, 1).reshape(B, S, H * Dh)


def _ctx_attention(q, k, v):
    B, L, H, Dh = q.shape
    s = jnp.einsum('bqhd,bkhd->bhqk', q, k).astype(jnp.float32) * Dh ** -0.5
    p = jax.nn.softmax(s, axis=-1).astype(v.dtype)
    return jnp.einsum('bhqk,bkhd->bqhd', p, v).reshape(B, L, H * Dh)


def _fourier(u):
    B, N, C = u.shape
    ug = u.reshape(B, N, FOURIER_GROUPS, FOURIER_GROUP_DIM).astype(jnp.float32)
    f = jnp.fft.fft2(ug, axes=(1, 3), norm='ortho').real
    return f.reshape(B, N, C).astype(u.dtype)


def _merge(o_attn, u_f, g_a, g_b, w_na, w_four, w_out):
    y_a = o_attn @ w_na
    y_f = _fourier(u_f) @ w_four
    return (jax.nn.sigmoid(g_a) * y_a + jax.nn.sigmoid(g_b) * y_f) @ w_out


def _route(h, w_router, b_router):
    N = h.shape[0]
    s = jax.nn.sigmoid((h @ w_router).astype(jnp.float32))
    s_sel = s + b_router.astype(jnp.float32)
    grp_score = lax.top_k(s_sel.reshape(N, N_GROUPS, N_EXPERTS // N_GROUPS), 2)[0].sum(-1)
    _, top_g = lax.top_k(grp_score, TOPK_GROUPS)
    gmask = jnp.any(top_g[..., None] == jnp.arange(N_GROUPS), axis=-2)
    emask = jnp.repeat(gmask, N_EXPERTS // N_GROUPS, axis=-1)
    _, idx = lax.top_k(jnp.where(emask, s_sel, -jnp.inf), TOP_K)
    w = jnp.take_along_axis(s, idx, axis=-1)
    w = w / jnp.sum(w, axis=-1, keepdims=True) * ROUTED_SCALE
    return idx, w.astype(h.dtype)


def _moe(h, w_router, b_router, w_eg, w_eu, w_ed, w_sg, w_su, w_sd):
    B, N0, D = h.shape
    hf = h.reshape(B * N0, D)
    N = hf.shape[0]
    idx, w = _route(hf, w_router, b_router)
    A = N * TOP_K
    e_flat = idx.reshape(A)
    t_flat = jnp.repeat(jnp.arange(N, dtype=jnp.int32), TOP_K)
    g_flat = w.reshape(A)
    order = jnp.argsort(e_flat)
    e_sorted = e_flat[order]
    counts = jnp.bincount(e_flat, length=N_EXPERTS)
    start = jnp.cumsum(counts) - counts
    padded = (counts + EXPERT_BLOCK - 1) // EXPERT_BLOCK * EXPERT_BLOCK
    pend = jnp.cumsum(padded)
    pstart = pend - padded
    dest = pstart[e_sorted] + (jnp.arange(A) - start[e_sorted])
    n_blocks = (A + N_EXPERTS * (EXPERT_BLOCK - 1) + EXPERT_BLOCK - 1) // EXPERT_BLOCK
    P = n_blocks * EXPERT_BLOCK
    tok_buf = jnp.full((P,), N, dtype=jnp.int32).at[dest].set(t_flat[order])
    gate_buf = jnp.zeros((P,), hf.dtype).at[dest].set(g_flat[order])
    blk_expert = jnp.minimum(jnp.searchsorted(pend, jnp.arange(n_blocks) * EXPERT_BLOCK, side='right'),
                             N_EXPERTS - 1)
    h_pad = jnp.concatenate([hf, jnp.zeros((1, D), hf.dtype)], axis=0)

    def expert_block(args):
        tok, g, e = args
        xb = h_pad[tok]
        a = jax.nn.silu(xb @ w_eg[e]) * (xb @ w_eu[e])
        return (a @ w_ed[e]) * g[:, None]

    y = lax.map(expert_block, (tok_buf.reshape(n_blocks, EXPERT_BLOCK),
                               gate_buf.reshape(n_blocks, EXPERT_BLOCK), blk_expert))
    routed = jnp.zeros((N + 1, D), hf.dtype).at[tok_buf].add(y.reshape(P, D))[:N]
    shared = (jax.nn.silu(hf @ w_sg) * (hf @ w_su)) @ w_sd
    return (routed + shared).reshape(B, N0, D)


def setup_inputs(seed: int = 0) -> dict:
    key = jax.random.key(seed)
    ks = jax.random.split(key, 24)
    D, L = D_MODEL, DEPTH

    def nrm(k, shape, scale):
        return jax.random.normal(k, shape, jnp.float32) * scale

    return {
        "x": nrm(ks[0], (BATCH, SEQ, D), 1.0),
        "c": nrm(ks[1], (BATCH, D), 1.0),
        "ctx": nrm(ks[2], (BATCH, CTX_LEN, D), 1.0),
        "c_ctx": nrm(ks[3], (D,), 1.0),
        "w_ada": nrm(ks[4], (L, D, N_MOD * D), 0.5 * D ** -0.5),
        "b_ada": nrm(ks[5], (L, N_MOD * D), 0.02),
        "g_mix": 1.0 + nrm(ks[6], (L, D), 0.05),
        "w_in": nrm(ks[7], (L, D, D_PROJ), D ** -0.5),
        "na_rpb": nrm(ks[8], (L, N_HEADS_NA, 2 * NA_KH - 1, 2 * NA_KW - 1), 0.1),
        "w_na": nrm(ks[9], (L, D_NA, D), D_NA ** -0.5),
        "w_four": nrm(ks[10], (L, D_FOURIER, D), D_FOURIER ** -0.5),
        "w_out": nrm(ks[11], (L, D, D), D ** -0.5),
        "g_ffn": 1.0 + nrm(ks[12], (L, D), 0.05),
        "w_router": nrm(ks[13], (L, D, N_EXPERTS), D ** -0.5),
        "b_router": nrm(ks[14], (L, N_EXPERTS), 0.01),
        "w_exp_gate": nrm(ks[15], (L, N_EXPERTS, D, D_EXPERT), D ** -0.5),
        "w_exp_up": nrm(ks[16], (L, N_EXPERTS, D, D_EXPERT), D ** -0.5),
        "w_exp_down": nrm(ks[17], (L, N_EXPERTS, D_EXPERT, D), D_EXPERT ** -0.5),
        "w_sh_gate": nrm(ks[18], (L, D, D_EXPERT), D ** -0.5),
        "w_sh_up": nrm(ks[19], (L, D, D_EXPERT), D ** -0.5),
        "w_sh_down": nrm(ks[20], (L, D_EXPERT, D), D_EXPERT ** -0.5),
        "g_final": 1.0 + nrm(ks[21], (D,), 0.05),
    }


def reference(x, c, ctx, c_ctx, w_ada, b_ada, g_mix, w_in, na_rpb, w_na, w_four, w_out, g_ffn,
              w_router, b_router, w_exp_gate, w_exp_up, w_exp_down, w_sh_gate, w_sh_up, w_sh_down,
              g_final):
    for l in range(DEPTH):
        update_ctx = l < DEPTH - 1
        mod = jax.nn.silu(c) @ w_ada[l] + b_ada[l]
        mod_c = jax.nn.silu(c_ctx) @ w_ada[l] + b_ada[l]
        sh1, sc1, ga1, sh2, sc2, ga2 = [m[:, None, :] for m in jnp.split(mod, N_MOD, axis=-1)]
        csh1, csc1, cga1, csh2, csc2, cga2 = jnp.split(mod_c, N_MOD, axis=-1)
        moe_w = (w_router[l], b_router[l], w_exp_gate[l], w_exp_up[l], w_exp_down[l],
                 w_sh_gate[l], w_sh_up[l], w_sh_down[l])

        h = _modulate(_rmsnorm(x, g_mix[l]), sh1, sc1)
        hc = _modulate(_rmsnorm(ctx, g_mix[l]), csh1, csc1)
        q, k, v, u_f, g_a, g_b = _split_proj(h @ w_in[l])
        qc, kc, vc, u_fc, g_ac, g_bc = _split_proj(hc @ w_in[l])
        o_na = _na_latent(_heads(q), _heads(k), _heads(v), _heads(kc), _heads(vc), na_rpb[l])
        x = x + ga1 * _merge(o_na, u_f, g_a, g_b, w_na[l], w_four[l], w_out[l])
        if update_ctx:
            o_na_c = _ctx_attention(_heads(qc), _heads(kc), _heads(vc))
            ctx = ctx + cga1 * _merge(o_na_c, u_fc, g_ac, g_bc, w_na[l], w_four[l], w_out[l])

        h2 = _modulate(_rmsnorm(x, g_ffn[l]), sh2, sc2)
        x = x + ga2 * _moe(h2, *moe_w)
        if update_ctx:
            hc2 = _modulate(_rmsnorm(ctx, g_ffn[l]), csh2, csc2)
            ctx = ctx + cga2 * _moe(hc2, *moe_w)
    return _rmsnorm(x, g_final)
```

```python
import functools

import numpy as np
import jax
import jax.numpy as jnp
from jax import lax
from jax.experimental import pallas as pl
from jax.experimental.pallas import tpu as pltpu

F32 = jnp.float32
BF16 = jnp.bfloat16

GRID_W = 64
N_HEADS = 16
HEAD_DIM = 64
D_NA = N_HEADS * HEAD_DIM
NA_KH = 8
NA_KW = 16
F_GROUPS = 8
F_GROUP_DIM = 128
D_FOURIER = F_GROUPS * F_GROUP_DIM
N_EXPERT_GROUPS = 8
TOPK_GROUPS = 4
TOP_K = 8
ROUTED_SCALE = 2.5
EPS = 1e-6
NEG_MASK = -1e30

LANES = 128
EXPERT_ROWS = 256
MIB = 1024 * 1024


def _cparams(sem, vmem_mib):
    return pltpu.CompilerParams(dimension_semantics=sem, vmem_limit_bytes=vmem_mib * MIB)


def _sigmoid(v):
    return 1.0 / (1.0 + jnp.exp(-v))


def _split_bf16(a):
    hi = a.astype(BF16)
    lo = (a - hi.astype(F32)).astype(BF16)
    return hi, lo


def _dot3(a, b, dims):
    ah, al = _split_bf16(a)
    bh, bl = _split_bf16(b)
    d = lambda p, q: lax.dot_general(p, q, dims, preferred_element_type=F32)
    return d(ah, bh) + (d(ah, bl) + d(al, bh))


_NN = (((1,), (0,)), ((), ()))
_NT = (((1,), (1,)), ((), ()))


def _ada_kernel(c_ref, w_ref, b_ref, o_ref):
    c = c_ref[...]
    a = c * _sigmoid(c)
    o_ref[...] = _dot3(a, w_ref[...], _NN) + b_ref[...]


def _ada(c8, w_ada, b_ada):
    d, n = w_ada.shape
    tn = 512
    return pl.pallas_call(
        _ada_kernel,
        out_shape=jax.ShapeDtypeStruct((8, n), F32),
        grid=(n // tn,),
        in_specs=[pl.BlockSpec((8, d), lambda j: (0, 0)),
                  pl.BlockSpec((d, tn), lambda j: (0, j)),
                  pl.BlockSpec((1, tn), lambda j: (0, j))],
        out_specs=pl.BlockSpec((8, tn), lambda j: (0, j)),
        compiler_params=_cparams(("parallel",), 48),
        name="ada",
    )(c8, w_ada, b_ada)


def _proj_kernel(x_ref, g_ref, sc_ref, sh_ref, w_ref, o_ref, h_ref):
    @pl.when(pl.program_id(2) == 0)
    def _():
        x = x_ref[0]
        ms = jnp.mean(x * x, axis=-1, keepdims=True)
        y = (x * lax.rsqrt(ms + EPS)) * g_ref[...]
        h_ref[...] = (y * (1.0 + sc_ref[0]) + sh_ref[0]).astype(BF16)

    o_ref[0] = jnp.dot(h_ref[...], w_ref[...], preferred_element_type=F32).astype(o_ref.dtype)


def _proj(x, g, sc, sh, w, col0, ncols, tm, tn):
    b, s, d = x.shape
    j0 = col0 // tn
    return pl.pallas_call(
        _proj_kernel,
        out_shape=jax.ShapeDtypeStruct((b, s, ncols), BF16),
        grid=(b, s // tm, ncols // tn),
        in_specs=[pl.BlockSpec((1, tm, d), lambda bi, i, j: (bi, i, 0)),
                  pl.BlockSpec((1, d), lambda bi, i, j: (0, 0)),
                  pl.BlockSpec((1, 1, d), lambda bi, i, j: (bi, 0, 0)),
                  pl.BlockSpec((1, 1, d), lambda bi, i, j: (bi, 0, 0)),
                  pl.BlockSpec((d, tn), lambda bi, i, j: (0, j + j0))],
        out_specs=pl.BlockSpec((1, tm, tn), lambda bi, i, j: (bi, i, j)),
        scratch_shapes=[pltpu.VMEM((tm, d), BF16)],
        compiler_params=_cparams(("parallel", "parallel", "arbitrary"), 48),
        name="proj",
    )(x, g, sc, sh, w)


NA_ROWS_PER_STEP = 8
NA_TOK = NA_ROWS_PER_STEP * GRID_W


def _na_bias_table(rpb):
    q = np.arange(GRID_W)
    kc = np.arange(GRID_W)
    ws = np.clip(q - NA_KW // 2, 0, GRID_W - NA_KW)
    inwin = (kc[None, :] >= ws[:, None]) & (kc[None, :] < ws[:, None] + NA_KW)
    cidx = np.clip(kc[None, :] - q[:, None], -(NA_KW - 1), NA_KW - 1) + NA_KW - 1
    c = rpb[:, :, cidx].astype(F32)
    c = jnp.where(jnp.asarray(inwin)[None, None], c, NEG_MASK)
    c2 = jnp.concatenate([c[:, :-1], c[:, 1:]], axis=-1)
    nd = c2.shape[1]
    c2 = c2.reshape(N_HEADS // 2, 2, nd, GRID_W, 2 * GRID_W).transpose(0, 2, 1, 3, 4)
    return c2.reshape(N_HEADS // 2, nd, 2 * GRID_W, 2 * GRID_W)


def _na_kernel(q_ref, kp_ref, kc_ref, kn_ref, vp_ref, vc_ref, vn_ref, kx_ref, vx_ref, bias_ref,
               o_ref, kbuf, vbuf, *, rows):
    t = pl.program_id(2)
    kbuf[0:NA_TOK] = kp_ref[0]
    kbuf[NA_TOK:2 * NA_TOK] = kc_ref[0]
    kbuf[2 * NA_TOK:3 * NA_TOK] = kn_ref[0]
    vbuf[0:NA_TOK] = vp_ref[0]
    vbuf[NA_TOK:2 * NA_TOK] = vc_ref[0]
    vbuf[2 * NA_TOK:3 * NA_TOK] = vn_ref[0]
    even = lax.broadcasted_iota(jnp.int32, (GRID_W, LANES), 1) < HEAD_DIM
    kx = kx_ref[0]
    vx = vx_ref[0]
    nwin = NA_KH * GRID_W

    def body(i, carry):
        r = t * NA_ROWS_PER_STEP + i
        rs = jnp.clip(r - NA_KH // 2, 0, rows - NA_KH)
        off = pl.multiple_of((rs - (t - 1) * NA_ROWS_PER_STEP) * GRID_W, GRID_W)
        d0 = rs - r + NA_KH - 1
        q = q_ref[0, pl.ds(pl.multiple_of(i * GRID_W, GRID_W), GRID_W), :]
        zero = jnp.zeros_like(q)
        qq = jnp.concatenate([jnp.where(even, q, zero), jnp.where(even, zero, q)], axis=0) * 0.125
        kl = kbuf[pl.ds(off, nwin), :]
        s_lat = lax.dot_general(qq, kl, _NT, preferred_element_type=F32)
        s_ctx = lax.dot_general(qq, kx, _NT, preferred_element_type=F32)
        bias = jnp.concatenate([bias_ref[0, d0 + 2 * j] for j in range(NA_KH // 2)], axis=1)
        s_lat = s_lat + bias
        m = jnp.maximum(jnp.max(s_lat, axis=1, keepdims=True), jnp.max(s_ctx, axis=1, keepdims=True))
        p_lat = jnp.exp(s_lat - m)
        p_ctx = jnp.exp(s_ctx - m)
        l = jnp.sum(p_lat, axis=1, keepdims=True) + jnp.sum(p_ctx, axis=1, keepdims=True)
        vl = vbuf[pl.ds(off, nwin), :]
        o2 = (jnp.dot(p_lat.astype(BF16), vl, preferred_element_type=F32)
              + jnp.dot(p_ctx.astype(BF16), vx, preferred_element_type=F32)) / l
        o = jnp.where(even, o2[0:GRID_W], o2[GRID_W:2 * GRID_W])
        o_ref[0, pl.ds(pl.multiple_of(i * GRID_W, GRID_W), GRID_W), :] = o.astype(o_ref.dtype)
        return carry

    lax.fori_loop(0, NA_ROWS_PER_STEP, body, 0)


def _na_attention(proj, kvx, bias):
    b, s, _ = proj.shape
    lctx = kvx.shape[1]
    rows = s // GRID_W
    ngrp = rows // NA_ROWS_PER_STEP
    npair = N_HEADS // 2
    kcol, vcol = D_NA // LANES, 2 * D_NA // LANES
    blk = (1, NA_TOK, LANES)
    prev = lambda t: jnp.maximum(t - 1, 0)
    nxt = lambda t: jnp.minimum(t + 1, ngrp - 1)
    return pl.pallas_call(
        functools.partial(_na_kernel, rows=rows),
        out_shape=jax.ShapeDtypeStruct((b, s, D_NA), BF16),
        grid=(b, npair, ngrp),
        in_specs=[pl.BlockSpec(blk, lambda bi, p, t: (bi, t, p)),
                  pl.BlockSpec(blk, lambda bi, p, t: (bi, prev(t), kcol + p)),
                  pl.BlockSpec(blk, lambda bi, p, t: (bi, t, kcol + p)),
                  pl.BlockSpec(blk, lambda bi, p, t: (bi, nxt(t), kcol + p)),
                  pl.BlockSpec(blk, lambda bi, p, t: (bi, prev(t), vcol + p)),
                  pl.BlockSpec(blk, lambda bi, p, t: (bi, t, vcol + p)),
                  pl.BlockSpec(blk, lambda bi, p, t: (bi, nxt(t), vcol + p)),
                  pl.BlockSpec((1, lctx, LANES), lambda bi, p, t: (bi, 0, p)),
                  pl.BlockSpec((1, lctx, LANES), lambda bi, p, t: (bi, 0, kcol + p)),
                  pl.BlockSpec((1,) + bias.shape[1:], lambda bi, p, t: (p, 0, 0, 0))],
        out_specs=pl.BlockSpec(blk, lambda bi, p, t: (bi, t, p)),
        scratch_shapes=[pltpu.VMEM((3 * NA_TOK, LANES), BF16), pltpu.VMEM((3 * NA_TOK, LANES), BF16)],
        compiler_params=_cparams(("parallel", "parallel", "arbitrary"), 32),
        name="na",
    )(proj, proj, proj, proj, proj, proj, proj, kvx, kvx, bias)


def _dft_consts(n1, n2):
    n = n1 * n2
    a1 = 2.0 * np.pi * np.outer(np.arange(n1), np.arange(n1)) / n1
    fa = np.concatenate([np.cos(a1), -np.sin(a1)], axis=0)
    tw = 2.0 * np.pi * np.outer(np.arange(n2), np.arange(n1)) / n
    twc = np.repeat(np.cos(tw)[:, :, None], LANES, axis=2)
    tws = np.repeat(np.sin(tw)[:, :, None], LANES, axis=2)
    a2 = 2.0 * np.pi * np.outer(np.arange(n2), np.arange(n2)) / n2
    c2, s2 = np.cos(a2), np.sin(a2)
    fb = np.block([[c2, s2], [-s2, c2]])
    ac = 2.0 * np.pi * np.outer(np.arange(F_GROUP_DIM), np.arange(F_GROUP_DIM)) / F_GROUP_DIM
    cs = np.concatenate([np.cos(ac), np.sin(ac)], axis=0) / np.sqrt(n * F_GROUP_DIM)
    f = lambda v: jnp.asarray(v.astype(np.float32))
    return f(fa), f(twc), f(tws), f(fb), f(cs)


def _dft_a_kernel(x_ref, fa_ref, c_ref, s_ref, o_ref, *, n1):
    y = jnp.dot(fa_ref[...].astype(BF16), x_ref[0, 0], preferred_element_type=F32)
    c = c_ref[0]
    s = s_ref[0]
    for g in range(D_FOURIER // LANES):
        sl = slice(g * LANES, (g + 1) * LANES)
        yr, yi = y[0:n1, sl], y[n1:2 * n1, sl]
        o_ref[0, 0, 0:n1, sl] = (yr * c + yi * s).astype(o_ref.dtype)
        o_ref[0, 0, n1:2 * n1, sl] = (yi * c - yr * s).astype(o_ref.dtype)


def _dft_b_kernel(y_ref, fb_ref, cs_ref, o_ref, *, n2):
    v = jnp.dot(fb_ref[...].astype(BF16), y_ref[0, 0], preferred_element_type=F32)
    cc = cs_ref[0:F_GROUP_DIM, :].astype(BF16)
    sc = cs_ref[F_GROUP_DIM:2 * F_GROUP_DIM, :].astype(BF16)
    for g in range(F_GROUPS):
        sl = slice(g * F_GROUP_DIM, (g + 1) * F_GROUP_DIM)
        vr = v[0:n2, sl].astype(BF16)
        vi = v[n2:2 * n2, sl].astype(BF16)
        o_ref[0, 0, :, sl] = (jnp.dot(vr, cc, preferred_element_type=F32)
                              + jnp.dot(vi, sc, preferred_element_type=F32)).astype(o_ref.dtype)


def _fourier(u):
    b, s, c = u.shape
    n2 = GRID_W
    n1 = s // n2
    fa, twc, tws, fb, cs = _dft_consts(n1, n2)
    xt = u.reshape(b, n1, n2, c).transpose(0, 2, 1, 3)
    ya = pl.pallas_call(
        functools.partial(_dft_a_kernel, n1=n1),
        out_shape=jax.ShapeDtypeStruct((b, n2, 2 * n1, c), BF16),
        grid=(b, n2),
        in_specs=[pl.BlockSpec((1, 1, n1, c), lambda bi, j: (bi, j, 0, 0)),
                  pl.BlockSpec((2 * n1, n1), lambda bi, j: (0, 0)),
                  pl.BlockSpec((1, n1, LANES), lambda bi, j: (j, 0, 0)),
                  pl.BlockSpec((1, n1, LANES), lambda bi, j: (j, 0, 0))],
        out_specs=pl.BlockSpec((1, 1, 2 * n1, c), lambda bi, j: (bi, j, 0, 0)),
        compiler_params=_cparams(("parallel", "parallel"), 32),
        name="dft_a",
    )(xt, fa, twc, tws)
    yt = ya.reshape(b, n2, 2, n1, c).transpose(0, 3, 2, 1, 4).reshape(b, n1, 2 * n2, c)
    vb = pl.pallas_call(
        functools.partial(_dft_b_kernel, n2=n2),
        out_shape=jax.ShapeDtypeStruct((b, n1, n2, c), BF16),
        grid=(b, n1),
        in_specs=[pl.BlockSpec((1, 1, 2 * n2, c), lambda bi, j: (bi, j, 0, 0)),
                  pl.BlockSpec((2 * n2, 2 * n2), lambda bi, j: (0, 0)),
                  pl.BlockSpec((2 * F_GROUP_DIM, F_GROUP_DIM), lambda bi, j: (0, 0))],
        out_specs=pl.BlockSpec((1, 1, n2, c), lambda bi, j: (bi, j, 0, 0)),
        compiler_params=_cparams(("parallel", "parallel"), 32),
        name="dft_b",
    )(yt, fb, cs)
    return vb.transpose(0, 2, 1, 3).reshape(b, s, c)


def _merge_kernel(o_ref, f_ref, ga_ref, gb_ref, x_ref, g1_ref, sc_ref, sh_ref, gf_ref,
                  wna_ref, wf_ref, wo_ref, x1_ref, h2_ref):
    ya = jnp.dot(o_ref[0], wna_ref[...], preferred_element_type=F32)
    yf = jnp.dot(f_ref[0], wf_ref[...], preferred_element_type=F32)
    m = _sigmoid(ga_ref[0].astype(F32)) * ya + _sigmoid(gb_ref[0].astype(F32)) * yf
    out = jnp.dot(m.astype(BF16), wo_ref[...], preferred_element_type=F32)
    x1 = x_ref[0] + g1_ref[0] * out
    x1_ref[0] = x1
    ms = jnp.mean(x1 * x1, axis=-1, keepdims=True)
    y = (x1 * lax.rsqrt(ms + EPS)) * gf_ref[...]
    h2_ref[0] = y * (1.0 + sc_ref[0]) + sh_ref[0]


def _merge(o_na, four, proj, x, ga1, sc2, sh2, g_ffn, w_na, w_four, w_out):
    b, s, d = x.shape
    tm = 256
    ga_blk = (D_NA * 3 + D_FOURIER) // d
    row = lambda bi, i: (bi, i, 0)
    per_b = lambda bi, i: (bi, 0, 0)
    full = lambda bi, i: (0, 0)
    return pl.pallas_call(
        _merge_kernel,
        out_shape=(jax.ShapeDtypeStruct((b, s, d), F32), jax.ShapeDtypeStruct((b, s, d), F32)),
        grid=(b, s // tm),
        in_specs=[pl.BlockSpec((1, tm, D_NA), row),
                  pl.BlockSpec((1, tm, D_FOURIER), row),
                  pl.BlockSpec((1, tm, d), lambda bi, i: (bi, i, ga_blk)),
                  pl.BlockSpec((1, tm, d), lambda bi, i: (bi, i, ga_blk + 1)),
                  pl.BlockSpec((1, tm, d), row),
                  pl.BlockSpec((1, 1, d), per_b),
                  pl.BlockSpec((1, 1, d), per_b),
                  pl.BlockSpec((1, 1, d), per_b),
                  pl.BlockSpec((1, d), full),
                  pl.BlockSpec((D_NA, d), full),
                  pl.BlockSpec((D_FOURIER, d), full),
                  pl.BlockSpec((d, d), full)],
        out_specs=(pl.BlockSpec((1, tm, d), row), pl.BlockSpec((1, tm, d), row)),
        compiler_params=_cparams(("parallel", "parallel"), 56),
        name="merge",
    )(o_na, four, proj, proj, x, ga1, sc2, sh2, g_ffn, w_na, w_four, w_out)


def _router_kernel(h_ref, wt_ref, b_ref, idx_ref, w_ref, *, n_exp):
    tm = h_ref.shape[0]
    gsz = n_exp // N_EXPERT_GROUPS
    s = _sigmoid(_dot3(wt_ref[...], h_ref[...], _NT))
    ssel = s + b_ref[...]
    rowf = lax.broadcasted_iota(jnp.int32, (n_exp, tm), 0).astype(F32)
    big = float(n_exp)
    neg = -jnp.inf
    cmax = lambda v: jnp.max(v, axis=0, keepdims=True)
    cmin = lambda v: jnp.min(v, axis=0, keepdims=True)

    gs = []
    for g in range(N_EXPERT_GROUPS):
        xg = ssel[g * gsz:(g + 1) * gsz]
        ig = (lax.broadcasted_iota(jnp.int32, (gsz, tm), 0) + g * gsz).astype(F32)
        m1 = cmax(xg)
        i1 = cmin(jnp.where(xg == m1, ig, big))
        m2 = cmax(jnp.where(ig == i1, neg, xg))
        gs.append(m1 + m2)
    parts = []
    for g in range(N_EXPERT_GROUPS):
        beaten = jnp.zeros((1, tm), F32)
        for o in range(N_EXPERT_GROUPS):
            if o != g:
                wins = (gs[o] >= gs[g]) if o < g else (gs[o] > gs[g])
                beaten = beaten + jnp.where(wins, 1.0, 0.0)
        keep = beaten < float(TOPK_GROUPS)
        parts.append(jnp.where(keep, ssel[g * gsz:(g + 1) * gsz], neg))
    cand = jnp.concatenate(parts, axis=0)
    ws = []
    for k in range(TOP_K):
        m = cmax(cand)
        i = cmin(jnp.where(cand == m, rowf, big))
        hit = rowf == i
        ws.append(jnp.sum(jnp.where(hit, s, 0.0), axis=0, keepdims=True))
        idx_ref[k:k + 1, :] = i.astype(jnp.int32)
        cand = jnp.where(hit, neg, cand)
    den = ws[0]
    for k in range(1, TOP_K):
        den = den + ws[k]
    for k in range(TOP_K):
        w_ref[k:k + 1, :] = ws[k] / den * ROUTED_SCALE


def _router(h2, w_router_t, b_router):
    n, d = h2.shape
    n_exp = w_router_t.shape[0]
    tm = 512
    return pl.pallas_call(
        functools.partial(_router_kernel, n_exp=n_exp),
        out_shape=(jax.ShapeDtypeStruct((TOP_K, n), jnp.int32), jax.ShapeDtypeStruct((TOP_K, n), F32)),
        grid=(n // tm,),
        in_specs=[pl.BlockSpec((tm, d), lambda i: (i, 0)),
                  pl.BlockSpec((n_exp, d), lambda i: (0, 0)),
                  pl.BlockSpec((n_exp, 1), lambda i: (0, 0))],
        out_specs=(pl.BlockSpec((TOP_K, tm), lambda i: (0, i)), pl.BlockSpec((TOP_K, tm), lambda i: (0, i))),
        compiler_params=_cparams(("parallel",), 32),
        name="router",
    )(h2, w_router_t, b_router)


def _gather_rows(idx_ref, n, src_hbm, dst, sem):
    def body(i, carry):
        t = idx_ref[0, 0, i]
        pltpu.make_async_copy(src_hbm.at[pl.ds(t, 1)], dst.at[pl.ds(i, 1)], sem).start()
        return carry
    lax.fori_loop(0, n, body, 0, unroll=8)


def _expert_kernel(be_ref, nu_ref, tokc_ref, tokn_ref, h_hbm, wg_ref, wu_ref, wd_ref, o_ref, xbuf, sem):
    b = pl.program_id(0)
    nu = nu_ref[0]
    slot = b % 2
    tb = xbuf.shape[1]

    @pl.when(jnp.logical_and(b == 0, nu > 0))
    def _():
        _gather_rows(tokc_ref, tb, h_hbm, xbuf.at[0], sem.at[0])

    @pl.when(b + 1 < nu)
    def _():
        _gather_rows(tokn_ref, tb, h_hbm, xbuf.at[1 - slot], sem.at[1 - slot])

    @pl.when(b < nu)
    def _():
        pltpu.make_async_copy(h_hbm.at[pl.ds(0, tb)], xbuf.at[slot], sem.at[slot]).wait()
        x = xbuf[slot].astype(BF16)
        g = jnp.dot(x, wg_ref[...].astype(BF16), preferred_element_type=F32)
        u = jnp.dot(x, wu_ref[...].astype(BF16), preferred_element_type=F32)
        a = (g * _sigmoid(g)) * u
        o_ref[...] = jnp.dot(a.astype(BF16), wd_ref[...].astype(BF16), preferred_element_type=F32)

    @pl.when(b >= nu)
    def _():
        o_ref[...] = jnp.zeros_like(o_ref)


def _experts(h2, tok3, blk_expert, n_used, w_eg, w_eu, w_ed):
    n, d = h2.shape
    nb, _, tb = tok3.shape
    n_exp, _, f = w_eg.shape
    grid_spec = pltpu.PrefetchScalarGridSpec(
        num_scalar_prefetch=2,
        grid=(nb,),
        in_specs=[pl.BlockSpec((1, 1, tb), lambda b, be, nu: (b, 0, 0), memory_space=pltpu.SMEM),
                  pl.BlockSpec((1, 1, tb), lambda b, be, nu: (jnp.minimum(b + 1, nb - 1), 0, 0),
                               memory_space=pltpu.SMEM),
                  pl.BlockSpec(memory_space=pl.ANY),
                  pl.BlockSpec((None, d, f), lambda b, be, nu: (be[b], 0, 0)),
                  pl.BlockSpec((None, d, f), lambda b, be, nu: (be[b], 0, 0)),
                  pl.BlockSpec((None, f, d), lambda b, be, nu: (be[b], 0, 0))],
        out_specs=pl.BlockSpec((tb, d), lambda b, be, nu: (b, 0)),
        scratch_shapes=[pltpu.VMEM((2, tb, d), F32), pltpu.SemaphoreType.DMA((2,))],
    )
    return pl.pallas_call(
        _expert_kernel,
        out_shape=jax.ShapeDtypeStruct((nb * tb, d), F32),
        grid_spec=grid_spec,
        compiler_params=_cparams(("arbitrary",), 56),
        name="experts",
    )(blk_expert, n_used, tok3, tok3, h2, w_eg, w_eu, w_ed)


def _shared_kernel(h_ref, wg_ref, wu_ref, wd_ref, o_ref):
    x = h_ref[...].astype(BF16)
    g = jnp.dot(x, wg_ref[...], preferred_element_type=F32)
    u = jnp.dot(x, wu_ref[...], preferred_element_type=F32)
    a = (g * _sigmoid(g)) * u
    o_ref[...] = jnp.dot(a.astype(BF16), wd_ref[...], preferred_element_type=F32)


def _shared(h2, w_sg, w_su, w_sd):
    n, d = h2.shape
    f = w_sg.shape[1]
    tm = 512
    return pl.pallas_call(
        _shared_kernel,
        out_shape=jax.ShapeDtypeStruct((n, d), F32),
        grid=(n // tm,),
        in_specs=[pl.BlockSpec((tm, d), lambda i: (i, 0)),
                  pl.BlockSpec((d, f), lambda i: (0, 0)),
                  pl.BlockSpec((d, f), lambda i: (0, 0)),
                  pl.BlockSpec((f, d), lambda i: (0, 0))],
        out_specs=pl.BlockSpec((tm, d), lambda i: (i, 0)),
        compiler_params=_cparams(("parallel",), 40),
        name="shared",
    )(h2, w_sg, w_su, w_sd)


COMBINE_ROWS = 128


def _combine_kernel(posc_ref, posn_ref, ys_hbm, w_ref, sh_ref, x1_ref, g2_ref, gf_ref, o_ref, buf, sem):
    j = pl.program_id(0)
    nj = pl.num_programs(0)
    slot = j % 2
    nrow = buf.shape[1]

    @pl.when(j == 0)
    def _():
        _gather_rows(posc_ref, nrow, ys_hbm, buf.at[0], sem.at[0])

    @pl.when(j + 1 < nj)
    def _():
        _gather_rows(posn_ref, nrow, ys_hbm, buf.at[1 - slot], sem.at[1 - slot])

    pltpu.make_async_copy(ys_hbm.at[pl.ds(0, nrow)], buf.at[slot], sem.at[slot]).wait()
    tm = o_ref.shape[0]
    w = w_ref[...]
    acc = w[:, 0:1] * buf[slot, 0:tm]
    for k in range(1, TOP_K):
        acc = acc + w[:, k:k + 1] * buf[slot, k * tm:(k + 1) * tm]
    x2 = x1_ref[...] + g2_ref[0] * (acc + sh_ref[...])
    ms = jnp.mean(x2 * x2, axis=-1, keepdims=True)
    o_ref[...] = (x2 * lax.rsqrt(ms + EPS)) * gf_ref[...]


def _combine(pos3, ys, w_tok, shared, x1, ga2, g_final, seq):
    n, d = x1.shape
    tm = COMBINE_ROWS
    nj = n // tm
    nrow = TOP_K * tm
    smem = lambda f: pl.BlockSpec((1, 1, nrow), f, memory_space=pltpu.SMEM)
    return pl.pallas_call(
        _combine_kernel,
        out_shape=jax.ShapeDtypeStruct((n, d), F32),
        grid=(nj,),
        in_specs=[smem(lambda j: (j, 0, 0)),
                  smem(lambda j: (jnp.minimum(j + 1, nj - 1), 0, 0)),
                  pl.BlockSpec(memory_space=pl.ANY),
                  pl.BlockSpec((tm, TOP_K), lambda j: (j, 0)),
                  pl.BlockSpec((tm, d), lambda j: (j, 0)),
                  pl.BlockSpec((tm, d), lambda j: (j, 0)),
                  pl.BlockSpec((1, 1, d), lambda j: ((j * tm) // seq, 0, 0)),
                  pl.BlockSpec((1, d), lambda j: (0, 0))],
        out_specs=pl.BlockSpec((tm, d), lambda j: (j, 0)),
        scratch_shapes=[pltpu.VMEM((2, nrow, d), F32), pltpu.SemaphoreType.DMA((2,))],
        compiler_params=_cparams(("arbitrary",), 40),
        name="combine",
    )(pos3, pos3, ys, w_tok, shared, x1, ga2, g_final)


def _routing_tables(idx_kn, n_exp, tb):
    k, n = idx_kn.shape
    a = n * k
    e_flat = idx_kn.T.reshape(a)
    order = jnp.argsort(e_flat)
    e_sorted = e_flat[order]
    counts = jnp.bincount(e_flat, length=n_exp)
    start = jnp.cumsum(counts) - counts
    padded = (counts + tb - 1) // tb * tb
    pend = jnp.cumsum(padded)
    pstart = pend - padded
    dest = (pstart[e_sorted] + (jnp.arange(a) - start[e_sorted])).astype(jnp.int32)
    nb = (a + n_exp * (tb - 1)) // tb
    tok = jnp.zeros((nb * tb,), jnp.int32).at[dest].set((order // k).astype(jnp.int32))
    pos = jnp.zeros((a,), jnp.int32).at[order].set(dest)
    blk_expert = jnp.minimum(jnp.searchsorted(pend, jnp.arange(nb) * tb, side='right'), n_exp - 1)
    n_used = (pend[-1] // tb).astype(jnp.int32).reshape(1)
    return tok.reshape(nb, 1, tb), pos, blk_expert.astype(jnp.int32), n_used


def kernel(x, c, ctx, c_ctx, w_ada, b_ada, g_mix, w_in, na_rpb, w_na, w_four, w_out, g_ffn, w_router, b_router,
           w_exp_gate, w_exp_up, w_exp_down, w_sh_gate, w_sh_up, w_sh_down, g_final):
    depth = w_ada.shape[0]
    assert depth == 1, "single-layer kernel: the context stream is never updated"
    b, s, d = x.shape
    n = b * s
    n_exp = w_router.shape[-1]
    rows = s // GRID_W
    assert s % (GRID_W * NA_ROWS_PER_STEP) == 0 and rows >= NA_KH and c.shape[0] + 1 <= 8

    c8 = jnp.zeros((8, d), F32).at[:b].set(c).at[b].set(c_ctx)
    mod = _ada(c8, w_ada[0], b_ada[0][None])
    sh1, sc1, ga1, sh2, sc2, ga2 = [m[:b, None, :] for m in jnp.split(mod, 6, axis=-1)]
    csh1, csc1 = [jnp.broadcast_to(m[b][None, None, :], (b, 1, d)) for m in jnp.split(mod, 6, axis=-1)[:2]]

    w_in_b = w_in[0].astype(BF16)
    g_mix2 = g_mix[0][None]
    proj = _proj(x, g_mix2, sc1, sh1, w_in_b, 0, w_in_b.shape[1], 1024, 512)
    kvx = _proj(ctx, g_mix2, csc1, csh1, w_in_b, D_NA, 2 * D_NA, ctx.shape[1], 512)

    o_na = _na_attention(proj, kvx, _na_bias_table(na_rpb[0]))
    four = _fourier(proj[:, :, 3 * D_NA:3 * D_NA + D_FOURIER])

    x1, h2 = _merge(o_na, four, proj, x, ga1, sc2, sh2, g_ffn[0][None],
                    w_na[0].astype(BF16), w_four[0].astype(BF16), w_out[0].astype(BF16))
    x1 = x1.reshape(n, d)
    h2 = h2.reshape(n, d)

    idx_kn, w_kn = _router(h2, w_router[0].T, b_router[0][:, None])
    tok3, pos, blk_expert, n_used = _routing_tables(idx_kn, n_exp, EXPERT_ROWS)
    ys = _experts(h2, tok3, blk_expert, n_used, w_exp_gate[0], w_exp_up[0], w_exp_down[0])
    shared = _shared(h2, w_sh_gate[0].astype(BF16), w_sh_up[0].astype(BF16), w_sh_down[0].astype(BF16))

    nj = n // COMBINE_ROWS
    pos3 = pos.reshape(nj, COMBINE_ROWS, TOP_K).transpose(0, 2, 1).reshape(nj, 1, TOP_K * COMBINE_ROWS)
    out = _combine(pos3, ys, w_kn.T, shared, x1, ga2, g_final[None], s)
    return out.reshape(b, s, d)
```

```python
import functools

import numpy as np
import jax
import jax.numpy as jnp
from jax import lax
from jax.experimental import pallas as pl
from jax.experimental.pallas import tpu as pltpu

F32 = jnp.float32
BF16 = jnp.bfloat16

GRID_W = 64
N_HEADS = 16
HEAD_DIM = 64
D_NA = N_HEADS * HEAD_DIM
NA_KH = 8
NA_KW = 16
F_GROUPS = 8
F_GROUP_DIM = 128
D_FOURIER = F_GROUPS * F_GROUP_DIM
N_EXPERT_GROUPS = 8
TOPK_GROUPS = 4
TOP_K = 8
ROUTED_SCALE = 2.5
EPS = 1e-6
NEG_MASK = -1e30

LANES = 128
SUBLANES = 8
EXPERT_ROWS = 256
MIB = 1024 * 1024


def _cparams(sem, vmem_mib):
    return pltpu.CompilerParams(dimension_semantics=sem, vmem_limit_bytes=vmem_mib * MIB)


def _sigmoid(v):
    return 1.0 / (1.0 + jnp.exp(-v))


def _split_bf16(a):
    hi = a.astype(BF16)
    lo = (a - hi.astype(F32)).astype(BF16)
    return hi, lo


def _dot3(a, b, dims):
    ah, al = _split_bf16(a)
    bh, bl = _split_bf16(b)
    d = lambda p, q: lax.dot_general(p, q, dims, preferred_element_type=F32)
    return d(ah, bh) + (d(ah, bl) + d(al, bh))


_NN = (((1,), (0,)), ((), ()))
_NT = (((1,), (1,)), ((), ()))


def _ada_kernel(c_ref, w_ref, b_ref, o_ref):
    c = c_ref[...]
    a = c * _sigmoid(c)
    o_ref[...] = _dot3(a, w_ref[...], _NN) + b_ref[...]


def _ada(c8, w_ada, b_ada):
    d, n = w_ada.shape
    tn = 512
    return pl.pallas_call(
        _ada_kernel,
        out_shape=jax.ShapeDtypeStruct((8, n), F32),
        grid=(n // tn,),
        in_specs=[pl.BlockSpec((8, d), lambda j: (0, 0)),
                  pl.BlockSpec((d, tn), lambda j: (0, j)),
                  pl.BlockSpec((1, tn), lambda j: (0, j))],
        out_specs=pl.BlockSpec((8, tn), lambda j: (0, j)),
        compiler_params=_cparams(("parallel",), 48),
        name="ada",
    )(c8, w_ada, b_ada)


NORM_ROWS = 32


def _proj_kernel(x_ref, g_ref, sc_ref, sh_ref, w_ref, o_ref, h_ref):
    @pl.when(pl.program_id(2) == 0)
    def _():
        gain = g_ref[...]
        scale = 1.0 + sc_ref[0]
        shift = sh_ref[0]

        def rows(c, carry):
            r = pl.multiple_of(c * NORM_ROWS, NORM_ROWS)
            x = x_ref[0, pl.ds(r, NORM_ROWS), :]
            ms = jnp.mean(x * x, axis=-1, keepdims=True)
            y = (x * lax.rsqrt(ms + EPS)) * gain
            h_ref[pl.ds(r, NORM_ROWS), :] = (y * scale + shift).astype(BF16)
            return carry

        lax.fori_loop(0, h_ref.shape[0] // NORM_ROWS, rows, 0)

    o_ref[0] = jnp.dot(h_ref[...], w_ref[...], preferred_element_type=F32).astype(o_ref.dtype)


def _proj(x, g, sc, sh, w, col0, ncols, tm, tn):
    b, s, d = x.shape
    j0 = col0 // tn
    return pl.pallas_call(
        _proj_kernel,
        out_shape=jax.ShapeDtypeStruct((b, s, ncols), BF16),
        grid=(b, s // tm, ncols // tn),
        in_specs=[pl.BlockSpec((1, tm, d), lambda bi, i, j: (bi, i, 0)),
                  pl.BlockSpec((1, d), lambda bi, i, j: (0, 0)),
                  pl.BlockSpec((1, 1, d), lambda bi, i, j: (bi, 0, 0)),
                  pl.BlockSpec((1, 1, d), lambda bi, i, j: (bi, 0, 0)),
                  pl.BlockSpec((d, tn), lambda bi, i, j: (0, j + j0))],
        out_specs=pl.BlockSpec((1, tm, tn), lambda bi, i, j: (bi, i, j)),
        scratch_shapes=[pltpu.VMEM((tm, d), BF16)],
        compiler_params=_cparams(("parallel", "parallel", "arbitrary"), 48),
        name="proj",
    )(x, g, sc, sh, w)


NA_ROWS_PER_STEP = 8
NA_TOK = NA_ROWS_PER_STEP * GRID_W


def _na_bias_table(rpb):
    q = np.arange(GRID_W)
    kc = np.arange(GRID_W)
    ws = np.clip(q - NA_KW // 2, 0, GRID_W - NA_KW)
    inwin = (kc[None, :] >= ws[:, None]) & (kc[None, :] < ws[:, None] + NA_KW)
    ext = jnp.pad(rpb.astype(F32), ((0, 0), (0, 0), (GRID_W - NA_KW, GRID_W - NA_KW)), mode="edge")
    c = jnp.stack([ext[:, :, GRID_W - 1 - qi:2 * GRID_W - 1 - qi] for qi in range(GRID_W)], axis=2)
    c = jnp.where(jnp.asarray(inwin)[None, None], c, NEG_MASK)
    c2 = jnp.concatenate([c[:, :-1], c[:, 1:]], axis=-1)
    nd = c2.shape[1]
    c2 = c2.reshape(N_HEADS // 2, 2, nd, GRID_W, 2 * GRID_W).transpose(0, 2, 1, 3, 4)
    return c2.reshape(N_HEADS // 2, nd, 2 * GRID_W, 2 * GRID_W)


def _na_kernel(q_ref, kp_ref, kc_ref, kn_ref, vp_ref, vc_ref, vn_ref, kx_ref, vx_ref, bias_ref,
               o_ref, kbuf, vbuf, q_scr, s_scr, p_scr, *, rows):
    t = pl.program_id(2)
    kbuf[0:NA_TOK] = kp_ref[0]
    kbuf[NA_TOK:2 * NA_TOK] = kc_ref[0]
    kbuf[2 * NA_TOK:3 * NA_TOK] = kn_ref[0]
    vbuf[0:NA_TOK] = vp_ref[0]
    vbuf[NA_TOK:2 * NA_TOK] = vc_ref[0]
    vbuf[2 * NA_TOK:3 * NA_TOK] = vn_ref[0]
    even = lax.broadcasted_iota(jnp.int32, (GRID_W, LANES), 1) < HEAD_DIM
    kx = kx_ref[0]
    vx = vx_ref[0]
    nwin = NA_KH * GRID_W
    qrows = 2 * GRID_W

    offs = []
    for i in range(NA_ROWS_PER_STEP):
        q = q_ref[0, i * GRID_W:(i + 1) * GRID_W, :]
        zero = jnp.zeros_like(q)
        q_scr[i * qrows:i * qrows + GRID_W] = jnp.where(even, q, zero) * 0.125
        q_scr[i * qrows + GRID_W:(i + 1) * qrows] = jnp.where(even, zero, q) * 0.125
    s_scr[:, nwin:] = lax.dot_general(q_scr[...], kx, _NT, preferred_element_type=F32)
    for i in range(NA_ROWS_PER_STEP):
        r = t * NA_ROWS_PER_STEP + i
        rs = jnp.clip(r - NA_KH // 2, 0, rows - NA_KH)
        off = pl.multiple_of((rs - (t - 1) * NA_ROWS_PER_STEP) * GRID_W, GRID_W)
        offs.append(off)
        d0 = rs - r + NA_KH - 1
        s_lat = lax.dot_general(q_scr[i * qrows:(i + 1) * qrows], kbuf[pl.ds(off, nwin), :], _NT,
                                preferred_element_type=F32)
        bias = jnp.concatenate([bias_ref[0, d0 + 2 * j] for j in range(NA_KH // 2)], axis=1)
        s_scr[i * qrows:(i + 1) * qrows, 0:nwin] = s_lat + bias
    inv_l = []
    for i in range(NA_ROWS_PER_STEP):
        s = s_scr[i * qrows:(i + 1) * qrows, :]
        p = jnp.exp(s - jnp.max(s, axis=1, keepdims=True))
        inv_l.append(1.0 / jnp.sum(p, axis=1, keepdims=True))
        p_scr[i * qrows:(i + 1) * qrows, :] = p.astype(BF16)
    o_ctx = jnp.dot(p_scr[:, nwin:], vx, preferred_element_type=F32)
    for i in range(NA_ROWS_PER_STEP):
        o2 = jnp.dot(p_scr[i * qrows:(i + 1) * qrows, 0:nwin], vbuf[pl.ds(offs[i], nwin), :],
                     preferred_element_type=F32)
        o2 = (o2 + o_ctx[i * qrows:(i + 1) * qrows]) * inv_l[i]
        o = jnp.where(even, o2[0:GRID_W], o2[GRID_W:qrows])
        o_ref[0, i * GRID_W:(i + 1) * GRID_W, :] = o.astype(o_ref.dtype)


def _na_attention(proj, kvx, bias):
    b, s, _ = proj.shape
    lctx = kvx.shape[1]
    rows = s // GRID_W
    ngrp = rows // NA_ROWS_PER_STEP
    npair = N_HEADS // 2
    kcol, vcol = D_NA // LANES, 2 * D_NA // LANES
    blk = (1, NA_TOK, LANES)
    prev = lambda t: jnp.maximum(t - 1, 0)
    nxt = lambda t: jnp.minimum(t + 1, ngrp - 1)
    return pl.pallas_call(
        functools.partial(_na_kernel, rows=rows),
        out_shape=jax.ShapeDtypeStruct((b, s, D_NA), BF16),
        grid=(b, npair, ngrp),
        in_specs=[pl.BlockSpec(blk, lambda bi, p, t: (bi, t, p)),
                  pl.BlockSpec(blk, lambda bi, p, t: (bi, prev(t), kcol + p)),
                  pl.BlockSpec(blk, lambda bi, p, t: (bi, t, kcol + p)),
                  pl.BlockSpec(blk, lambda bi, p, t: (bi, nxt(t), kcol + p)),
                  pl.BlockSpec(blk, lambda bi, p, t: (bi, prev(t), vcol + p)),
                  pl.BlockSpec(blk, lambda bi, p, t: (bi, t, vcol + p)),
                  pl.BlockSpec(blk, lambda bi, p, t: (bi, nxt(t), vcol + p)),
                  pl.BlockSpec((1, lctx, LANES), lambda bi, p, t: (bi, 0, p)),
                  pl.BlockSpec((1, lctx, LANES), lambda bi, p, t: (bi, 0, kcol + p)),
                  pl.BlockSpec((1,) + bias.shape[1:], lambda bi, p, t: (p, 0, 0, 0))],
        out_specs=pl.BlockSpec(blk, lambda bi, p, t: (bi, t, p)),
        scratch_shapes=[pltpu.VMEM((3 * NA_TOK, LANES), BF16), pltpu.VMEM((3 * NA_TOK, LANES), BF16),
                        pltpu.VMEM((2 * NA_TOK, LANES), BF16),
                        pltpu.VMEM((2 * NA_TOK, NA_KH * GRID_W + lctx), F32),
                        pltpu.VMEM((2 * NA_TOK, NA_KH * GRID_W + lctx), BF16)],
        compiler_params=_cparams(("parallel", "parallel", "arbitrary"), 32),
        name="na",
    )(proj, proj, proj, proj, proj, proj, proj, kvx, kvx, bias)


def _dft_consts(n1, n2):
    n = n1 * n2
    a1 = 2.0 * np.pi * np.outer(np.arange(n1), np.arange(n1)) / n1
    fa = np.concatenate([np.cos(a1), -np.sin(a1)], axis=0)
    tw = 2.0 * np.pi * np.outer(np.arange(n2), np.arange(n1)) / n
    twc = np.repeat(np.cos(tw)[:, :, None], LANES, axis=2)
    tws = np.repeat(np.sin(tw)[:, :, None], LANES, axis=2)
    a2 = 2.0 * np.pi * np.outer(np.arange(n2), np.arange(n2)) / n2
    c2, s2 = np.cos(a2), np.sin(a2)
    fb = np.block([[c2, s2], [-s2, c2]])
    ac = 2.0 * np.pi * np.outer(np.arange(F_GROUP_DIM), np.arange(F_GROUP_DIM)) / F_GROUP_DIM
    cs = np.concatenate([np.cos(ac), np.sin(ac)], axis=0) / np.sqrt(n * F_GROUP_DIM)
    f = lambda v: jnp.asarray(v.astype(np.float32))
    return f(fa), f(twc), f(tws), f(fb), f(cs)


DFT_STEP = 4


def _dft_a_kernel(x_ref, fa_ref, c_ref, s_ref, o_ref, *, n1):
    fa = fa_ref[...].astype(BF16)
    for j in range(DFT_STEP):
        y = jnp.dot(fa, x_ref[0, j], preferred_element_type=F32)
        c = c_ref[j]
        s = s_ref[j]
        for g in range(D_FOURIER // LANES):
            sl = slice(g * LANES, (g + 1) * LANES)
            yr, yi = y[0:n1, sl], y[n1:2 * n1, sl]
            o_ref[0, j, 0:n1, sl] = (yr * c + yi * s).astype(o_ref.dtype)
            o_ref[0, j, n1:2 * n1, sl] = (yi * c - yr * s).astype(o_ref.dtype)


def _dft_b_kernel(y_ref, fb_ref, cs_ref, o_ref, *, n2):
    fb = fb_ref[...].astype(BF16)
    cc = cs_ref[0:F_GROUP_DIM, :].astype(BF16)
    sc = cs_ref[F_GROUP_DIM:2 * F_GROUP_DIM, :].astype(BF16)
    for j in range(DFT_STEP):
        v = jnp.dot(fb, y_ref[0, j], preferred_element_type=F32)
        for g in range(F_GROUPS):
            sl = slice(g * F_GROUP_DIM, (g + 1) * F_GROUP_DIM)
            vr = v[0:n2, sl].astype(BF16)
            vi = v[n2:2 * n2, sl].astype(BF16)
            o_ref[0, j, :, sl] = (jnp.dot(vr, cc, preferred_element_type=F32)
                                  + jnp.dot(vi, sc, preferred_element_type=F32)).astype(o_ref.dtype)


def _fourier(u):
    b, s, c = u.shape
    n2 = GRID_W
    n1 = s // n2
    fa, twc, tws, fb, cs = _dft_consts(n1, n2)
    xt = u.reshape(b, n1, n2, c).transpose(0, 2, 1, 3)
    ya = pl.pallas_call(
        functools.partial(_dft_a_kernel, n1=n1),
        out_shape=jax.ShapeDtypeStruct((b, n2, 2 * n1, c), BF16),
        grid=(b, n2 // DFT_STEP),
        in_specs=[pl.BlockSpec((1, DFT_STEP, n1, c), lambda bi, j: (bi, j, 0, 0)),
                  pl.BlockSpec((2 * n1, n1), lambda bi, j: (0, 0)),
                  pl.BlockSpec((DFT_STEP, n1, LANES), lambda bi, j: (j, 0, 0)),
                  pl.BlockSpec((DFT_STEP, n1, LANES), lambda bi, j: (j, 0, 0))],
        out_specs=pl.BlockSpec((1, DFT_STEP, 2 * n1, c), lambda bi, j: (bi, j, 0, 0)),
        compiler_params=_cparams(("parallel", "parallel"), 32),
        name="dft_a",
    )(xt, fa, twc, tws)
    yt = ya.reshape(b, n2, 2, n1, c).transpose(0, 3, 2, 1, 4).reshape(b, n1, 2 * n2, c)
    vb = pl.pallas_call(
        functools.partial(_dft_b_kernel, n2=n2),
        out_shape=jax.ShapeDtypeStruct((b, n1, n2, c), BF16),
        grid=(b, n1 // DFT_STEP),
        in_specs=[pl.BlockSpec((1, DFT_STEP, 2 * n2, c), lambda bi, j: (bi, j, 0, 0)),
                  pl.BlockSpec((2 * n2, 2 * n2), lambda bi, j: (0, 0)),
                  pl.BlockSpec((2 * F_GROUP_DIM, F_GROUP_DIM), lambda bi, j: (0, 0))],
        out_specs=pl.BlockSpec((1, DFT_STEP, n2, c), lambda bi, j: (bi, j, 0, 0)),
        compiler_params=_cparams(("parallel", "parallel"), 32),
        name="dft_b",
    )(yt, fb, cs)
    return vb.transpose(0, 2, 1, 3).reshape(b, s, c)


def _merge_kernel(o_ref, f_ref, ga_ref, gb_ref, x_ref, g1_ref, sc_ref, sh_ref, gf_ref,
                  wna_ref, wf_ref, wo_ref, x1_ref, h2_ref):
    ya = jnp.dot(o_ref[0], wna_ref[...], preferred_element_type=F32)
    yf = jnp.dot(f_ref[0], wf_ref[...], preferred_element_type=F32)
    m = _sigmoid(ga_ref[0].astype(F32)) * ya + _sigmoid(gb_ref[0].astype(F32)) * yf
    out = jnp.dot(m.astype(BF16), wo_ref[...], preferred_element_type=F32)
    x1 = x_ref[0] + g1_ref[0] * out
    x1_ref[0] = x1
    ms = jnp.mean(x1 * x1, axis=-1, keepdims=True)
    y = (x1 * lax.rsqrt(ms + EPS)) * gf_ref[...]
    h2_ref[0] = y * (1.0 + sc_ref[0]) + sh_ref[0]


def _merge(o_na, four, proj, x, ga1, sc2, sh2, g_ffn, w_na, w_four, w_out):
    b, s, d = x.shape
    tm = 256
    ga_blk = (D_NA * 3 + D_FOURIER) // d
    row = lambda bi, i: (bi, i, 0)
    per_b = lambda bi, i: (bi, 0, 0)
    full = lambda bi, i: (0, 0)
    return pl.pallas_call(
        _merge_kernel,
        out_shape=(jax.ShapeDtypeStruct((b, s, d), F32), jax.ShapeDtypeStruct((b, s, d), F32)),
        grid=(b, s // tm),
        in_specs=[pl.BlockSpec((1, tm, D_NA), row),
                  pl.BlockSpec((1, tm, D_FOURIER), row),
                  pl.BlockSpec((1, tm, d), lambda bi, i: (bi, i, ga_blk)),
                  pl.BlockSpec((1, tm, d), lambda bi, i: (bi, i, ga_blk + 1)),
                  pl.BlockSpec((1, tm, d), row),
                  pl.BlockSpec((1, 1, d), per_b),
                  pl.BlockSpec((1, 1, d), per_b),
                  pl.BlockSpec((1, 1, d), per_b),
                  pl.BlockSpec((1, d), full),
                  pl.BlockSpec((D_NA, d), full),
                  pl.BlockSpec((D_FOURIER, d), full),
                  pl.BlockSpec((d, d), full)],
        out_specs=(pl.BlockSpec((1, tm, d), row), pl.BlockSpec((1, tm, d), row)),
        compiler_params=_cparams(("parallel", "parallel"), 56),
        name="merge",
    )(o_na, four, proj, proj, x, ga1, sc2, sh2, g_ffn, w_na, w_four, w_out)


def _router_kernel(h_ref, wt_ref, b_ref, idx_ref, w_ref, rank_ref, cnt_ref, base_scr, *, n_exp):
    tm = h_ref.shape[0]

    @pl.when(pl.program_id(0) == 0)
    def _():
        base_scr[...] = jnp.zeros_like(base_scr)

    gsz = n_exp // N_EXPERT_GROUPS
    s = _sigmoid(_dot3(wt_ref[...], h_ref[...], _NT))
    ssel = s + b_ref[...]
    rowf = lax.broadcasted_iota(jnp.int32, (n_exp, tm), 0).astype(F32)
    big = float(n_exp)
    neg = -jnp.inf
    cmax = lambda v: jnp.max(v, axis=0, keepdims=True)
    cmin = lambda v: jnp.min(v, axis=0, keepdims=True)

    gs = []
    for g in range(N_EXPERT_GROUPS):
        xg = ssel[g * gsz:(g + 1) * gsz]
        ig = (lax.broadcasted_iota(jnp.int32, (gsz, tm), 0) + g * gsz).astype(F32)
        m1 = cmax(xg)
        i1 = cmin(jnp.where(xg == m1, ig, big))
        m2 = cmax(jnp.where(ig == i1, neg, xg))
        gs.append(m1 + m2)
    parts = []
    for g in range(N_EXPERT_GROUPS):
        beaten = jnp.zeros((1, tm), F32)
        for o in range(N_EXPERT_GROUPS):
            if o != g:
                wins = (gs[o] >= gs[g]) if o < g else (gs[o] > gs[g])
                beaten = beaten + jnp.where(wins, 1.0, 0.0)
        keep = beaten < float(TOPK_GROUPS)
        parts.append(jnp.where(keep, ssel[g * gsz:(g + 1) * gsz], neg))
    cand = jnp.concatenate(parts, axis=0)
    ws, picks = [], []
    for k in range(TOP_K):
        m = cmax(cand)
        i = cmin(jnp.where(cand == m, rowf, big))
        hit = rowf == i
        ws.append(jnp.sum(jnp.where(hit, s, 0.0), axis=0, keepdims=True))
        picks.append(i)
        idx_ref[k:k + 1, :] = i.astype(jnp.int32)
        cand = jnp.where(hit, neg, cand)
    den = ws[0]
    for k in range(1, TOP_K):
        den = den + ws[k]
    for k in range(TOP_K):
        w_ref[k:k + 1, :] = ws[k] / den * ROUTED_SCALE

    hits = jnp.zeros((n_exp, tm), F32)
    for k in range(TOP_K):
        hits = hits + jnp.where(rowf == picks[k], 1.0, 0.0)
    earlier = (lax.broadcasted_iota(jnp.int32, (tm, tm), 0) < lax.broadcasted_iota(jnp.int32, (tm, tm), 1))
    before = jnp.dot(hits.astype(BF16), jnp.where(earlier, 1.0, 0.0).astype(BF16),
                     preferred_element_type=F32) + base_scr[...]
    for k in range(TOP_K):
        sel = rowf == picks[k]
        rank_ref[k:k + 1, :] = jnp.sum(jnp.where(sel, before, 0.0), axis=0, keepdims=True).astype(jnp.int32)
    base_scr[...] = base_scr[...] + jnp.sum(hits, axis=1, keepdims=True)
    cnt_ref[...] = base_scr[...]


def _router(h2, w_router_t, b_router):
    n, d = h2.shape
    n_exp = w_router_t.shape[0]
    tm = 512
    tok = lambda i: (0, i)
    return pl.pallas_call(
        functools.partial(_router_kernel, n_exp=n_exp),
        out_shape=(jax.ShapeDtypeStruct((TOP_K, n), jnp.int32), jax.ShapeDtypeStruct((TOP_K, n), F32),
                   jax.ShapeDtypeStruct((TOP_K, n), jnp.int32), jax.ShapeDtypeStruct((n_exp, 1), F32)),
        grid=(n // tm,),
        in_specs=[pl.BlockSpec((tm, d), lambda i: (i, 0)),
                  pl.BlockSpec((n_exp, d), lambda i: (0, 0)),
                  pl.BlockSpec((n_exp, 1), lambda i: (0, 0))],
        out_specs=(pl.BlockSpec((TOP_K, tm), tok), pl.BlockSpec((TOP_K, tm), tok), pl.BlockSpec((TOP_K, tm), tok),
                   pl.BlockSpec((n_exp, 1), lambda i: (0, 0))),
        scratch_shapes=[pltpu.VMEM((n_exp, 1), F32)],
        compiler_params=_cparams(("arbitrary",), 32),
        name="router",
    )(h2, w_router_t, b_router)


def _dest_kernel(idx_ref, rank_ref, ps_ref, o_ref, *, n_exp):
    tm = idx_ref.shape[1]
    rowf = lax.broadcasted_iota(jnp.int32, (n_exp, tm), 0).astype(F32)
    ps = ps_ref[...]
    for k in range(TOP_K):
        sel = rowf == idx_ref[k:k + 1, :].astype(F32)
        dest = jnp.sum(jnp.where(sel, ps, 0.0), axis=0, keepdims=True).astype(jnp.int32) + rank_ref[k:k + 1, :]
        for j in range(tm // LANES):
            o_ref[j, k:k + 1, :] = dest[:, j * LANES:(j + 1) * LANES]


def _dest(idx_kn, rank_kn, pstart_f):
    k, n = idx_kn.shape
    n_exp = pstart_f.shape[0]
    tm = 2048
    return pl.pallas_call(
        functools.partial(_dest_kernel, n_exp=n_exp),
        out_shape=jax.ShapeDtypeStruct((n // LANES, k, LANES), jnp.int32),
        grid=(n // tm,),
        in_specs=[pl.BlockSpec((k, tm), lambda i: (0, i)),
                  pl.BlockSpec((k, tm), lambda i: (0, i)),
                  pl.BlockSpec((n_exp, 1), lambda i: (0, 0))],
        out_specs=pl.BlockSpec((tm // LANES, k, LANES), lambda i: (i, 0, 0)),
        compiler_params=_cparams(("parallel",), 32),
        name="dest",
    )(idx_kn, rank_kn, pstart_f)


TOKEN_TILE = LANES


def _fill_unowned_rows(pad_start_ref, pad_len_ref, nu_ref, zeros, xs_hbm, zsem, tb, wait):
    n_exp = pad_start_ref.shape[0]
    nb = xs_hbm.shape[0] // tb

    def go(row, nrows):
        c = pltpu.make_async_copy(zeros.at[pl.ds(0, nrows)], xs_hbm.at[pl.ds(row, nrows)], zsem)
        c.wait() if wait else c.start()

    def per_expert(e, carry):
        row = pad_start_ref[e]
        left = pad_len_ref[e]
        singles = left & (SUBLANES - 1)
        for r in range(SUBLANES - 1):
            @pl.when(r < singles)
            def _(r=r):
                go(row + r, 1)
        row = row + singles
        chunk = SUBLANES
        while chunk < tb:
            @pl.when((left & chunk) != 0)
            def _(row=row, chunk=chunk):
                go(pl.multiple_of(row, chunk), chunk)
            row = row + (left & chunk)
            chunk *= 2
        return carry

    lax.fori_loop(0, n_exp, per_expert, 0)

    def per_tail_block(b, carry):
        go(pl.multiple_of(b * tb, tb), tb)
        return carry

    lax.fori_loop(nu_ref[0], nb, per_tail_block, 0)


def _dispatch_kernel(pad_start_ref, pad_len_ref, nu_ref, dest_ref, h_ref, xs_hbm, stage, zeros, sem, zsem, *, tb):
    j = pl.program_id(0)
    nj = pl.num_programs(0)
    slot = j % 2
    tm = h_ref.shape[0]

    def wait_slot(s):
        for _ in range(TOP_K):
            pltpu.make_async_copy(stage.at[s], xs_hbm.at[pl.ds(0, tm)], sem.at[s]).wait()

    @pl.when(j == 0)
    def _():
        zeros[...] = jnp.zeros_like(zeros)
        _fill_unowned_rows(pad_start_ref, pad_len_ref, nu_ref, zeros, xs_hbm, zsem, tb, wait=False)

    packed = h_ref[...]
    for s in range(2):
        @pl.when(slot == s)
        def _(s=s):
            @pl.when(j >= 2)
            def _():
                wait_slot(s)

            stage[s] = packed
            for i in range(tm):
                for k in range(TOP_K):
                    pltpu.make_async_copy(stage.at[s, pl.ds(i, 1)], xs_hbm.at[pl.ds(dest_ref[0, k, i], 1)],
                                          sem.at[s]).start(priority=k % 2)

    @pl.when(j == nj - 1)
    def _():
        wait_slot(slot)

    @pl.when(jnp.logical_and(j == nj - 1, nj >= 2))
    def _():
        wait_slot(1 - slot)

    @pl.when(j == nj - 1)
    def _():
        _fill_unowned_rows(pad_start_ref, pad_len_ref, nu_ref, zeros, xs_hbm, zsem, tb, wait=True)


def _dispatch(dest3, h2, pad_start, pad_len, n_used, n_rows, tb):
    n, d = h2.shape
    tm = TOKEN_TILE
    grid_spec = pltpu.PrefetchScalarGridSpec(
        num_scalar_prefetch=3,
        grid=(n // tm,),
        in_specs=[pl.BlockSpec((1, TOP_K, tm), lambda j, ps, pn, nu: (j, 0, 0), memory_space=pltpu.SMEM),
                  pl.BlockSpec((tm, d), lambda j, ps, pn, nu: (j, 0))],
        out_specs=pl.BlockSpec(memory_space=pl.ANY),
        scratch_shapes=[pltpu.VMEM((2, tm, d), F32), pltpu.VMEM((tb, d), F32),
                        pltpu.SemaphoreType.DMA((2,)), pltpu.SemaphoreType.DMA(())],
    )
    return pl.pallas_call(
        functools.partial(_dispatch_kernel, tb=tb),
        out_shape=jax.ShapeDtypeStruct((n_rows, d), F32),
        grid_spec=grid_spec,
        compiler_params=_cparams(("arbitrary",), 32),
        name="dispatch",
    )(pad_start, pad_len, n_used, dest3, h2)


def _expert_kernel(be_ref, first_ref, slot_ref, next_ref, nu_ref,
                   x_ref, wg_hbm, wu_hbm, wd_hbm, o_ref, wg_buf, wu_buf, wd_buf, sem):
    b = pl.program_id(0)
    nu = nu_ref[0]
    slot = slot_ref[b]

    def weight_copies(e, s):
        return (pltpu.make_async_copy(wg_hbm.at[e], wg_buf.at[s], sem.at[s, 0]),
                pltpu.make_async_copy(wu_hbm.at[e], wu_buf.at[s], sem.at[s, 1]),
                pltpu.make_async_copy(wd_hbm.at[e], wd_buf.at[s], sem.at[s, 2]))

    def start_weights(e, s):
        for i, c in enumerate(weight_copies(e, s)):
            c.start(priority=min(i, 1))

    @pl.when(b == 0)
    def _():
        start_weights(be_ref[0], slot)

    @pl.when(first_ref[b] == 1)
    def _():
        for c in weight_copies(be_ref[b], slot):
            c.wait()
        nxt = next_ref[b]

        @pl.when(nxt >= 0)
        def _():
            start_weights(nxt, 1 - slot)

    @pl.when(b < nu)
    def _():
        x = x_ref[...].astype(BF16)
        g = jnp.dot(x, wg_buf[slot].astype(BF16), preferred_element_type=F32)
        u = jnp.dot(x, wu_buf[slot].astype(BF16), preferred_element_type=F32)
        a = (g * _sigmoid(g)) * u
        o_ref[...] = jnp.dot(a.astype(BF16), wd_buf[slot].astype(BF16), preferred_element_type=F32)

    @pl.when(b >= nu)
    def _():
        o_ref[...] = jnp.zeros_like(o_ref)


def _experts(xs, blocks, w_eg, w_eu, w_ed, tb):
    p, dh = xs.shape
    nb = p // tb
    n_exp, d, f = w_eg.shape
    last_used = lambda b, nu: jnp.minimum(b, nu[0] - 1)
    grid_spec = pltpu.PrefetchScalarGridSpec(
        num_scalar_prefetch=5,
        grid=(nb,),
        in_specs=[pl.BlockSpec((tb, dh), lambda b, be, fi, sl, nx, nu: (last_used(b, nu), 0)),
                  pl.BlockSpec(memory_space=pl.ANY),
                  pl.BlockSpec(memory_space=pl.ANY),
                  pl.BlockSpec(memory_space=pl.ANY)],
        out_specs=pl.BlockSpec((tb, dh), lambda b, be, fi, sl, nx, nu: (b, 0)),
        scratch_shapes=[pltpu.VMEM((2, d, f), F32), pltpu.VMEM((2, d, f), F32), pltpu.VMEM((2, f, d), F32),
                        pltpu.SemaphoreType.DMA((2, 3))],
    )
    return pl.pallas_call(
        _expert_kernel,
        out_shape=jax.ShapeDtypeStruct((p, dh), F32),
        grid_spec=grid_spec,
        compiler_params=_cparams(("arbitrary",), 56),
        name="experts",
    )(*blocks, xs, w_eg, w_eu, w_ed)


def _shared_kernel(h_ref, wg_ref, wu_ref, wd_ref, o_ref):
    x = h_ref[...].astype(BF16)
    g = jnp.dot(x, wg_ref[...], preferred_element_type=F32)
    u = jnp.dot(x, wu_ref[...], preferred_element_type=F32)
    a = (g * _sigmoid(g)) * u
    o_ref[...] = jnp.dot(a.astype(BF16), wd_ref[...], preferred_element_type=F32)


def _shared(h2, w_sg, w_su, w_sd):
    n, d = h2.shape
    f = w_sg.shape[1]
    tm = 512
    return pl.pallas_call(
        _shared_kernel,
        out_shape=jax.ShapeDtypeStruct((n, d), F32),
        grid=(n // tm,),
        in_specs=[pl.BlockSpec((tm, d), lambda i: (i, 0)),
                  pl.BlockSpec((d, f), lambda i: (0, 0)),
                  pl.BlockSpec((d, f), lambda i: (0, 0)),
                  pl.BlockSpec((f, d), lambda i: (0, 0))],
        out_specs=pl.BlockSpec((tm, d), lambda i: (i, 0)),
        compiler_params=_cparams(("parallel",), 40),
        name="shared",
    )(h2, w_sg, w_su, w_sd)


def _gather_sorted_rows(dest_ref, ys_hbm, dst, sem):
    tm = dest_ref.shape[2]
    for k in range(TOP_K):
        for i in range(tm):
            pltpu.make_async_copy(ys_hbm.at[pl.ds(dest_ref[0, k, i], 1)], dst.at[pl.ds(k * tm + i, 1)],
                                  sem).start(priority=i % 2)


def _combine_kernel(dstc_ref, dstn_ref, ys_hbm, w_ref, sh_ref, x1_ref, g2_ref, gf_ref, o_ref, buf, sem):
    j = pl.program_id(0)
    nj = pl.num_programs(0)
    slot = j % 2
    tm, d = o_ref.shape

    @pl.when(j == 0)
    def _():
        _gather_sorted_rows(dstc_ref, ys_hbm, buf.at[0], sem.at[0])

    for s in range(2):
        @pl.when(jnp.logical_and(j + 1 < nj, slot == 1 - s))
        def _(s=s):
            _gather_sorted_rows(dstn_ref, ys_hbm, buf.at[s], sem.at[s])

    pltpu.make_async_copy(ys_hbm.at[pl.ds(0, TOP_K * tm)], buf.at[slot], sem.at[slot]).wait()
    w = w_ref[...]
    gf = gf_ref[...]
    g2 = g2_ref[0]
    rows = 32
    for r in range(0, tm, rows):
        acc = w[r:r + rows, 0:1] * buf[slot, r:r + rows]
        for k in range(1, TOP_K):
            acc = acc + w[r:r + rows, k:k + 1] * buf[slot, k * tm + r:k * tm + r + rows]
        x2 = x1_ref[r:r + rows, :] + g2 * (acc + sh_ref[r:r + rows, :])
        ms = jnp.mean(x2 * x2, axis=-1, keepdims=True)
        o_ref[r:r + rows, :] = (x2 * lax.rsqrt(ms + EPS)) * gf


def _combine(dest3, ys, w_tok, shared, x1, ga2, g_final, seq):
    n, d = x1.shape
    tm = TOKEN_TILE
    nj = n // tm
    smem = lambda f: pl.BlockSpec((1, TOP_K, tm), f, memory_space=pltpu.SMEM)
    return pl.pallas_call(
        _combine_kernel,
        out_shape=jax.ShapeDtypeStruct((n, d), F32),
        grid=(nj,),
        in_specs=[smem(lambda j: (j, 0, 0)),
                  smem(lambda j: (jnp.minimum(j + 1, nj - 1), 0, 0)),
                  pl.BlockSpec(memory_space=pl.ANY),
                  pl.BlockSpec((tm, TOP_K), lambda j: (j, 0)),
                  pl.BlockSpec((tm, d), lambda j: (j, 0)),
                  pl.BlockSpec((tm, d), lambda j: (j, 0)),
                  pl.BlockSpec((1, 1, d), lambda j: ((j * tm) // seq, 0, 0)),
                  pl.BlockSpec((1, d), lambda j: (0, 0))],
        out_specs=pl.BlockSpec((tm, d), lambda j: (j, 0)),
        scratch_shapes=[pltpu.VMEM((2, TOP_K * tm, d), F32), pltpu.SemaphoreType.DMA((2,))],
        compiler_params=_cparams(("arbitrary",), 40),
        name="combine",
    )(dest3, dest3, ys, w_tok, shared, x1, ga2, g_final)


def _block_tables(counts, tb, nb):
    n_exp = counts.shape[0]
    padded = (counts + tb - 1) // tb * tb
    pend = jnp.cumsum(padded)
    pstart = pend - padded
    blk_row = jnp.arange(nb, dtype=jnp.int32) * tb
    be = jnp.minimum(jnp.sum((pend[None, :] <= blk_row[:, None]).astype(jnp.int32), axis=1), n_exp - 1)
    n_used = pend[-1] // tb
    nonempty = counts > 0
    ordinal = jnp.cumsum(nonempty.astype(jnp.int32)) - 1
    ids = jnp.where(nonempty, jnp.arange(n_exp, dtype=jnp.int32), n_exp)
    later = jnp.flip(lax.cummin(jnp.flip(ids)))
    nxt = jnp.concatenate([later[1:], jnp.full((1,), n_exp, jnp.int32)])
    sel = (be[:, None] == jnp.arange(n_exp, dtype=jnp.int32)[None, :]).astype(jnp.int32)
    pick = lambda v: jnp.sum(sel * v[None, :], axis=1)
    b_start, b_ord, b_nxt = pick(pstart), pick(ordinal), pick(nxt)
    used = jnp.arange(nb) < n_used
    first = jnp.logical_and(blk_row == b_start, used).astype(jnp.int32)
    b_nxt = jnp.where(b_nxt < n_exp, b_nxt, -1)
    i32 = lambda v: v.astype(jnp.int32)
    blocks = (i32(be), first, i32(b_ord % 2), i32(b_nxt), i32(n_used).reshape(1))
    pads = (i32(pstart + counts), i32(padded - counts), i32(n_used).reshape(1))
    return pstart, blocks, pads


def kernel(x, c, ctx, c_ctx, w_ada, b_ada, g_mix, w_in, na_rpb, w_na, w_four, w_out, g_ffn, w_router, b_router,
           w_exp_gate, w_exp_up, w_exp_down, w_sh_gate, w_sh_up, w_sh_down, g_final):
    depth = w_ada.shape[0]
    assert depth == 1, "single-layer kernel: the context stream is never updated"
    b, s, d = x.shape
    n = b * s
    n_exp = w_router.shape[-1]
    rows = s // GRID_W
    assert s % (GRID_W * NA_ROWS_PER_STEP) == 0 and rows >= NA_KH and c.shape[0] + 1 <= 8

    c8 = jnp.zeros((8, d), F32).at[:b].set(c).at[b].set(c_ctx)
    mod = _ada(c8, w_ada[0], b_ada[0][None])
    sh1, sc1, ga1, sh2, sc2, ga2 = [m[:b, None, :] for m in jnp.split(mod, 6, axis=-1)]
    csh1, csc1 = [jnp.broadcast_to(m[b][None, None, :], (b, 1, d)) for m in jnp.split(mod, 6, axis=-1)[:2]]

    w_in_b = w_in[0].astype(BF16)
    g_mix2 = g_mix[0][None]
    proj = _proj(x, g_mix2, sc1, sh1, w_in_b, 0, w_in_b.shape[1], 1024, 512)
    kvx = _proj(ctx, g_mix2, csc1, csh1, w_in_b, D_NA, 2 * D_NA, ctx.shape[1], 512)

    o_na = _na_attention(proj, kvx, _na_bias_table(na_rpb[0]))
    four = _fourier(proj[:, :, 3 * D_NA:3 * D_NA + D_FOURIER])

    x1, h2 = _merge(o_na, four, proj, x, ga1, sc2, sh2, g_ffn[0][None],
                    w_na[0].astype(BF16), w_four[0].astype(BF16), w_out[0].astype(BF16))
    x1 = x1.reshape(n, d)
    h2 = h2.reshape(n, d)

    idx_kn, w_kn, rank_kn, counts = _router(h2, w_router[0].T, b_router[0][:, None])
    nb = (n * TOP_K + n_exp * (EXPERT_ROWS - 1)) // EXPERT_ROWS
    pstart, blocks, pads = _block_tables(counts[:, 0].astype(jnp.int32), EXPERT_ROWS, nb)
    dest3 = _dest(idx_kn, rank_kn, pstart.astype(F32)[:, None])
    xs = _dispatch(dest3, h2, *pads, nb * EXPERT_ROWS, EXPERT_ROWS)
    ys = _experts(xs, blocks, w_exp_gate[0], w_exp_up[0], w_exp_down[0], EXPERT_ROWS)
    shared = _shared(h2, w_sh_gate[0].astype(BF16), w_sh_up[0].astype(BF16), w_sh_down[0].astype(BF16))
    out = _combine(dest3, ys, w_kn.T, shared, x1, ga2, g_final[None], s)
    return out.reshape(b, s, d)
```

```python
import functools

import numpy as np
import jax
import jax.numpy as jnp
from jax import lax
from jax.experimental import pallas as pl
from jax.experimental.pallas import tpu as pltpu

F32 = jnp.float32
BF16 = jnp.bfloat16

GRID_W = 64
N_HEADS = 16
HEAD_DIM = 64
D_NA = N_HEADS * HEAD_DIM
NA_KH = 8
NA_KW = 16
F_GROUPS = 8
F_GROUP_DIM = 128
D_FOURIER = F_GROUPS * F_GROUP_DIM
N_EXPERT_GROUPS = 8
TOPK_GROUPS = 4
TOP_K = 8
ROUTED_SCALE = 2.5
EPS = 1e-6
NEG_MASK = -1e30

LANES = 128
SUBLANES = 8
EXPERT_ROWS = 256
MIB = 1024 * 1024


def _cparams(sem, vmem_mib):
    return pltpu.CompilerParams(dimension_semantics=sem, vmem_limit_bytes=vmem_mib * MIB)


def _sigmoid(v):
    return 1.0 / (1.0 + jnp.exp(-v))


def _split_bf16(a):
    hi = a.astype(BF16)
    lo = (a - hi.astype(F32)).astype(BF16)
    return hi, lo


def _dot3(a, b, dims):
    ah, al = _split_bf16(a)
    bh, bl = _split_bf16(b)
    d = lambda p, q: lax.dot_general(p, q, dims, preferred_element_type=F32)
    return d(ah, bh) + (d(ah, bl) + d(al, bh))


_NN = (((1,), (0,)), ((), ()))
_NT = (((1,), (1,)), ((), ()))


def _ada_kernel(c_ref, w_ref, b_ref, o_ref):
    c = c_ref[...]
    a = c * _sigmoid(c)
    o_ref[...] = _dot3(a, w_ref[...], _NN) + b_ref[...]


def _ada(c8, w_ada, b_ada):
    d, n = w_ada.shape
    tn = 512
    return pl.pallas_call(
        _ada_kernel,
        out_shape=jax.ShapeDtypeStruct((8, n), F32),
        grid=(n // tn,),
        in_specs=[pl.BlockSpec((8, d), lambda j: (0, 0)),
                  pl.BlockSpec((d, tn), lambda j: (0, j)),
                  pl.BlockSpec((1, tn), lambda j: (0, j))],
        out_specs=pl.BlockSpec((8, tn), lambda j: (0, j)),
        compiler_params=_cparams(("parallel",), 48),
        name="ada",
    )(c8, w_ada, b_ada)


def _proj_kernel(x_ref, g_ref, sc_ref, sh_ref, w_ref, o_ref, h_ref):
    @pl.when(pl.program_id(2) == 0)
    def _():
        x = x_ref[0]
        ms = jnp.mean(x * x, axis=-1, keepdims=True)
        y = (x * lax.rsqrt(ms + EPS)) * g_ref[...]
        h_ref[...] = (y * (1.0 + sc_ref[0]) + sh_ref[0]).astype(BF16)

    o_ref[0] = jnp.dot(h_ref[...], w_ref[...], preferred_element_type=F32).astype(o_ref.dtype)


def _proj(x, g, sc, sh, w, col0, ncols, tm, tn):
    b, s, d = x.shape
    j0 = col0 // tn
    return pl.pallas_call(
        _proj_kernel,
        out_shape=jax.ShapeDtypeStruct((b, s, ncols), BF16),
        grid=(b, s // tm, ncols // tn),
        in_specs=[pl.BlockSpec((1, tm, d), lambda bi, i, j: (bi, i, 0)),
                  pl.BlockSpec((1, d), lambda bi, i, j: (0, 0)),
                  pl.BlockSpec((1, 1, d), lambda bi, i, j: (bi, 0, 0)),
                  pl.BlockSpec((1, 1, d), lambda bi, i, j: (bi, 0, 0)),
                  pl.BlockSpec((d, tn), lambda bi, i, j: (0, j + j0))],
        out_specs=pl.BlockSpec((1, tm, tn), lambda bi, i, j: (bi, i, j)),
        scratch_shapes=[pltpu.VMEM((tm, d), BF16)],
        compiler_params=_cparams(("parallel", "parallel", "arbitrary"), 48),
        name="proj",
    )(x, g, sc, sh, w)


NA_ROWS_PER_STEP = 8
NA_TOK = NA_ROWS_PER_STEP * GRID_W


def _na_bias_table(rpb):
    q = np.arange(GRID_W)
    kc = np.arange(GRID_W)
    ws = np.clip(q - NA_KW // 2, 0, GRID_W - NA_KW)
    inwin = (kc[None, :] >= ws[:, None]) & (kc[None, :] < ws[:, None] + NA_KW)
    ext = jnp.pad(rpb.astype(F32), ((0, 0), (0, 0), (GRID_W - NA_KW, GRID_W - NA_KW)), mode="edge")
    c = jnp.stack([ext[:, :, GRID_W - 1 - qi:2 * GRID_W - 1 - qi] for qi in range(GRID_W)], axis=2)
    c = jnp.where(jnp.asarray(inwin)[None, None], c, NEG_MASK)
    c2 = jnp.concatenate([c[:, :-1], c[:, 1:]], axis=-1)
    nd = c2.shape[1]
    c2 = c2.reshape(N_HEADS // 2, 2, nd, GRID_W, 2 * GRID_W).transpose(0, 2, 1, 3, 4)
    return c2.reshape(N_HEADS // 2, nd, 2 * GRID_W, 2 * GRID_W)


def _na_kernel(q_ref, kp_ref, kc_ref, kn_ref, vp_ref, vc_ref, vn_ref, kx_ref, vx_ref, bias_ref,
               o_ref, kbuf, vbuf, q_scr, s_scr, p_scr, *, rows):
    t = pl.program_id(2)
    kbuf[0:NA_TOK] = kp_ref[0]
    kbuf[NA_TOK:2 * NA_TOK] = kc_ref[0]
    kbuf[2 * NA_TOK:3 * NA_TOK] = kn_ref[0]
    vbuf[0:NA_TOK] = vp_ref[0]
    vbuf[NA_TOK:2 * NA_TOK] = vc_ref[0]
    vbuf[2 * NA_TOK:3 * NA_TOK] = vn_ref[0]
    even = lax.broadcasted_iota(jnp.int32, (GRID_W, LANES), 1) < HEAD_DIM
    kx = kx_ref[0]
    vx = vx_ref[0]
    nwin = NA_KH * GRID_W
    qrows = 2 * GRID_W

    offs = []
    for i in range(NA_ROWS_PER_STEP):
        q = q_ref[0, i * GRID_W:(i + 1) * GRID_W, :]
        zero = jnp.zeros_like(q)
        q_scr[i * qrows:i * qrows + GRID_W] = jnp.where(even, q, zero) * 0.125
        q_scr[i * qrows + GRID_W:(i + 1) * qrows] = jnp.where(even, zero, q) * 0.125
    s_scr[:, nwin:] = lax.dot_general(q_scr[...], kx, _NT, preferred_element_type=F32)
    for i in range(NA_ROWS_PER_STEP):
        r = t * NA_ROWS_PER_STEP + i
        rs = jnp.clip(r - NA_KH // 2, 0, rows - NA_KH)
        off = pl.multiple_of((rs - (t - 1) * NA_ROWS_PER_STEP) * GRID_W, GRID_W)
        offs.append(off)
        d0 = rs - r + NA_KH - 1
        s_lat = lax.dot_general(q_scr[i * qrows:(i + 1) * qrows], kbuf[pl.ds(off, nwin), :], _NT,
                                preferred_element_type=F32)
        bias = jnp.concatenate([bias_ref[0, d0 + 2 * j] for j in range(NA_KH // 2)], axis=1)
        s_scr[i * qrows:(i + 1) * qrows, 0:nwin] = s_lat + bias
    inv_l = []
    for i in range(NA_ROWS_PER_STEP):
        s = s_scr[i * qrows:(i + 1) * qrows, :]
        p = jnp.exp(s - jnp.max(s, axis=1, keepdims=True))
        inv_l.append(1.0 / jnp.sum(p, axis=1, keepdims=True))
        p_scr[i * qrows:(i + 1) * qrows, :] = p.astype(BF16)
    o_ctx = jnp.dot(p_scr[:, nwin:], vx, preferred_element_type=F32)
    for i in range(NA_ROWS_PER_STEP):
        o2 = jnp.dot(p_scr[i * qrows:(i + 1) * qrows, 0:nwin], vbuf[pl.ds(offs[i], nwin), :],
                     preferred_element_type=F32)
        o2 = (o2 + o_ctx[i * qrows:(i + 1) * qrows]) * inv_l[i]
        o = jnp.where(even, o2[0:GRID_W], o2[GRID_W:qrows])
        o_ref[0, i * GRID_W:(i + 1) * GRID_W, :] = o.astype(o_ref.dtype)


def _na_attention(proj, kvx, bias):
    b, s, _ = proj.shape
    lctx = kvx.shape[1]
    rows = s // GRID_W
    ngrp = rows // NA_ROWS_PER_STEP
    npair = N_HEADS // 2
    kcol, vcol = D_NA // LANES, 2 * D_NA // LANES
    blk = (1, NA_TOK, LANES)
    prev = lambda t: jnp.maximum(t - 1, 0)
    nxt = lambda t: jnp.minimum(t + 1, ngrp - 1)
    return pl.pallas_call(
        functools.partial(_na_kernel, rows=rows),
        out_shape=jax.ShapeDtypeStruct((b, s, D_NA), BF16),
        grid=(b, npair, ngrp),
        in_specs=[pl.BlockSpec(blk, lambda bi, p, t: (bi, t, p)),
                  pl.BlockSpec(blk, lambda bi, p, t: (bi, prev(t), kcol + p)),
                  pl.BlockSpec(blk, lambda bi, p, t: (bi, t, kcol + p)),
                  pl.BlockSpec(blk, lambda bi, p, t: (bi, nxt(t), kcol + p)),
                  pl.BlockSpec(blk, lambda bi, p, t: (bi, prev(t), vcol + p)),
                  pl.BlockSpec(blk, lambda bi, p, t: (bi, t, vcol + p)),
                  pl.BlockSpec(blk, lambda bi, p, t: (bi, nxt(t), vcol + p)),
                  pl.BlockSpec((1, lctx, LANES), lambda bi, p, t: (bi, 0, p)),
                  pl.BlockSpec((1, lctx, LANES), lambda bi, p, t: (bi, 0, kcol + p)),
                  pl.BlockSpec((1,) + bias.shape[1:], lambda bi, p, t: (p, 0, 0, 0))],
        out_specs=pl.BlockSpec(blk, lambda bi, p, t: (bi, t, p)),
        scratch_shapes=[pltpu.VMEM((3 * NA_TOK, LANES), BF16), pltpu.VMEM((3 * NA_TOK, LANES), BF16),
                        pltpu.VMEM((2 * NA_TOK, LANES), BF16),
                        pltpu.VMEM((2 * NA_TOK, NA_KH * GRID_W + lctx), F32),
                        pltpu.VMEM((2 * NA_TOK, NA_KH * GRID_W + lctx), BF16)],
        compiler_params=_cparams(("parallel", "parallel", "arbitrary"), 32),
        name="na",
    )(proj, proj, proj, proj, proj, proj, proj, kvx, kvx, bias)


def _dft_consts(n1, n2):
    n = n1 * n2
    a1 = 2.0 * np.pi * np.outer(np.arange(n1), np.arange(n1)) / n1
    fa = np.concatenate([np.cos(a1), -np.sin(a1)], axis=0)
    tw = 2.0 * np.pi * np.outer(np.arange(n2), np.arange(n1)) / n
    twc = np.repeat(np.cos(tw)[:, :, None], LANES, axis=2)
    tws = np.repeat(np.sin(tw)[:, :, None], LANES, axis=2)
    a2 = 2.0 * np.pi * np.outer(np.arange(n2), np.arange(n2)) / n2
    c2, s2 = np.cos(a2), np.sin(a2)
    fb = np.block([[c2, s2], [-s2, c2]])
    ac = 2.0 * np.pi * np.outer(np.arange(F_GROUP_DIM), np.arange(F_GROUP_DIM)) / F_GROUP_DIM
    cs = np.concatenate([np.cos(ac), np.sin(ac)], axis=0) / np.sqrt(n * F_GROUP_DIM)
    f = lambda v: jnp.asarray(v.astype(np.float32))
    return f(fa), f(twc), f(tws), f(fb), f(cs)


DFT_STEP = 4


def _dft_a_kernel(x_ref, fa_ref, c_ref, s_ref, o_ref, *, n1):
    fa = fa_ref[...].astype(BF16)
    for j in range(DFT_STEP):
        y = jnp.dot(fa, x_ref[0, j], preferred_element_type=F32)
        c = c_ref[j]
        s = s_ref[j]
        for g in range(D_FOURIER // LANES):
            sl = slice(g * LANES, (g + 1) * LANES)
            yr, yi = y[0:n1, sl], y[n1:2 * n1, sl]
            o_ref[0, j, 0:n1, sl] = (yr * c + yi * s).astype(o_ref.dtype)
            o_ref[0, j, n1:2 * n1, sl] = (yi * c - yr * s).astype(o_ref.dtype)


def _dft_b_kernel(y_ref, fb_ref, cs_ref, o_ref, *, n2):
    fb = fb_ref[...].astype(BF16)
    cc = cs_ref[0:F_GROUP_DIM, :].astype(BF16)
    sc = cs_ref[F_GROUP_DIM:2 * F_GROUP_DIM, :].astype(BF16)
    for j in range(DFT_STEP):
        v = jnp.dot(fb, y_ref[0, j], preferred_element_type=F32)
        for g in range(F_GROUPS):
            sl = slice(g * F_GROUP_DIM, (g + 1) * F_GROUP_DIM)
            vr = v[0:n2, sl].astype(BF16)
            vi = v[n2:2 * n2, sl].astype(BF16)
            o_ref[0, j, :, sl] = (jnp.dot(vr, cc, preferred_element_type=F32)
                                  + jnp.dot(vi, sc, preferred_element_type=F32)).astype(o_ref.dtype)


def _fourier(u):
    b, s, c = u.shape
    n2 = GRID_W
    n1 = s // n2
    fa, twc, tws, fb, cs = _dft_consts(n1, n2)
    xt = u.reshape(b, n1, n2, c).transpose(0, 2, 1, 3)
    ya = pl.pallas_call(
        functools.partial(_dft_a_kernel, n1=n1),
        out_shape=jax.ShapeDtypeStruct((b, n2, 2 * n1, c), BF16),
        grid=(b, n2 // DFT_STEP),
        in_specs=[pl.BlockSpec((1, DFT_STEP, n1, c), lambda bi, j: (bi, j, 0, 0)),
                  pl.BlockSpec((2 * n1, n1), lambda bi, j: (0, 0)),
                  pl.BlockSpec((DFT_STEP, n1, LANES), lambda bi, j: (j, 0, 0)),
                  pl.BlockSpec((DFT_STEP, n1, LANES), lambda bi, j: (j, 0, 0))],
        out_specs=pl.BlockSpec((1, DFT_STEP, 2 * n1, c), lambda bi, j: (bi, j, 0, 0)),
        compiler_params=_cparams(("parallel", "parallel"), 32),
        name="dft_a",
    )(xt, fa, twc, tws)
    yt = ya.reshape(b, n2, 2, n1, c).transpose(0, 3, 2, 1, 4).reshape(b, n1, 2 * n2, c)
    vb = pl.pallas_call(
        functools.partial(_dft_b_kernel, n2=n2),
        out_shape=jax.ShapeDtypeStruct((b, n1, n2, c), BF16),
        grid=(b, n1 // DFT_STEP),
        in_specs=[pl.BlockSpec((1, DFT_STEP, 2 * n2, c), lambda bi, j: (bi, j, 0, 0)),
                  pl.BlockSpec((2 * n2, 2 * n2), lambda bi, j: (0, 0)),
                  pl.BlockSpec((2 * F_GROUP_DIM, F_GROUP_DIM), lambda bi, j: (0, 0))],
        out_specs=pl.BlockSpec((1, DFT_STEP, n2, c), lambda bi, j: (bi, j, 0, 0)),
        compiler_params=_cparams(("parallel", "parallel"), 32),
        name="dft_b",
    )(yt, fb, cs)
    return vb.transpose(0, 2, 1, 3).reshape(b, s, c)


def _merge_kernel(o_ref, f_ref, ga_ref, gb_ref, x_ref, g1_ref, sc_ref, sh_ref, gf_ref,
                  wna_ref, wf_ref, wo_ref, x1_ref, h2_ref):
    ya = jnp.dot(o_ref[0], wna_ref[...], preferred_element_type=F32)
    yf = jnp.dot(f_ref[0], wf_ref[...], preferred_element_type=F32)
    m = _sigmoid(ga_ref[0].astype(F32)) * ya + _sigmoid(gb_ref[0].astype(F32)) * yf
    out = jnp.dot(m.astype(BF16), wo_ref[...], preferred_element_type=F32)
    x1 = x_ref[0] + g1_ref[0] * out
    x1_ref[0] = x1
    ms = jnp.mean(x1 * x1, axis=-1, keepdims=True)
    y = (x1 * lax.rsqrt(ms + EPS)) * gf_ref[...]
    h2_ref[0] = y * (1.0 + sc_ref[0]) + sh_ref[0]


def _merge(o_na, four, proj, x, ga1, sc2, sh2, g_ffn, w_na, w_four, w_out):
    b, s, d = x.shape
    tm = 256
    ga_blk = (D_NA * 3 + D_FOURIER) // d
    row = lambda bi, i: (bi, i, 0)
    per_b = lambda bi, i: (bi, 0, 0)
    full = lambda bi, i: (0, 0)
    return pl.pallas_call(
        _merge_kernel,
        out_shape=(jax.ShapeDtypeStruct((b, s, d), F32), jax.ShapeDtypeStruct((b, s, d), F32)),
        grid=(b, s // tm),
        in_specs=[pl.BlockSpec((1, tm, D_NA), row),
                  pl.BlockSpec((1, tm, D_FOURIER), row),
                  pl.BlockSpec((1, tm, d), lambda bi, i: (bi, i, ga_blk)),
                  pl.BlockSpec((1, tm, d), lambda bi, i: (bi, i, ga_blk + 1)),
                  pl.BlockSpec((1, tm, d), row),
                  pl.BlockSpec((1, 1, d), per_b),
                  pl.BlockSpec((1, 1, d), per_b),
                  pl.BlockSpec((1, 1, d), per_b),
                  pl.BlockSpec((1, d), full),
                  pl.BlockSpec((D_NA, d), full),
                  pl.BlockSpec((D_FOURIER, d), full),
                  pl.BlockSpec((d, d), full)],
        out_specs=(pl.BlockSpec((1, tm, d), row), pl.BlockSpec((1, tm, d), row)),
        compiler_params=_cparams(("parallel", "parallel"), 56),
        name="merge",
    )(o_na, four, proj, proj, x, ga1, sc2, sh2, g_ffn, w_na, w_four, w_out)


def _router_kernel(h_ref, wt_ref, b_ref, idx_ref, w_ref, rank_ref, cnt_ref, base_scr, *, n_exp):
    tm = h_ref.shape[0]

    @pl.when(pl.program_id(0) == 0)
    def _():
        base_scr[...] = jnp.zeros_like(base_scr)

    gsz = n_exp // N_EXPERT_GROUPS
    s = _sigmoid(_dot3(wt_ref[...], h_ref[...], _NT))
    ssel = s + b_ref[...]
    rowf = lax.broadcasted_iota(jnp.int32, (n_exp, tm), 0).astype(F32)
    big = float(n_exp)
    neg = -jnp.inf
    cmax = lambda v: jnp.max(v, axis=0, keepdims=True)
    cmin = lambda v: jnp.min(v, axis=0, keepdims=True)

    gs = []
    for g in range(N_EXPERT_GROUPS):
        xg = ssel[g * gsz:(g + 1) * gsz]
        ig = (lax.broadcasted_iota(jnp.int32, (gsz, tm), 0) + g * gsz).astype(F32)
        m1 = cmax(xg)
        i1 = cmin(jnp.where(xg == m1, ig, big))
        m2 = cmax(jnp.where(ig == i1, neg, xg))
        gs.append(m1 + m2)
    parts = []
    for g in range(N_EXPERT_GROUPS):
        beaten = jnp.zeros((1, tm), F32)
        for o in range(N_EXPERT_GROUPS):
            if o != g:
                wins = (gs[o] >= gs[g]) if o < g else (gs[o] > gs[g])
                beaten = beaten + jnp.where(wins, 1.0, 0.0)
        keep = beaten < float(TOPK_GROUPS)
        parts.append(jnp.where(keep, ssel[g * gsz:(g + 1) * gsz], neg))
    cand = jnp.concatenate(parts, axis=0)
    ws, picks = [], []
    for k in range(TOP_K):
        m = cmax(cand)
        i = cmin(jnp.where(cand == m, rowf, big))
        hit = rowf == i
        ws.append(jnp.sum(jnp.where(hit, s, 0.0), axis=0, keepdims=True))
        picks.append(i)
        idx_ref[k:k + 1, :] = i.astype(jnp.int32)
        cand = jnp.where(hit, neg, cand)
    den = ws[0]
    for k in range(1, TOP_K):
        den = den + ws[k]
    for k in range(TOP_K):
        w_ref[k:k + 1, :] = ws[k] / den * ROUTED_SCALE

    hits = jnp.zeros((n_exp, tm), F32)
    for k in range(TOP_K):
        hits = hits + jnp.where(rowf == picks[k], 1.0, 0.0)
    earlier = (lax.broadcasted_iota(jnp.int32, (tm, tm), 0) < lax.broadcasted_iota(jnp.int32, (tm, tm), 1))
    before = jnp.dot(hits.astype(BF16), jnp.where(earlier, 1.0, 0.0).astype(BF16),
                     preferred_element_type=F32) + base_scr[...]
    for k in range(TOP_K):
        sel = rowf == picks[k]
        rank_ref[k:k + 1, :] = jnp.sum(jnp.where(sel, before, 0.0), axis=0, keepdims=True).astype(jnp.int32)
    base_scr[...] = base_scr[...] + jnp.sum(hits, axis=1, keepdims=True)
    cnt_ref[...] = base_scr[...]


def _router(h2, w_router_t, b_router):
    n, d = h2.shape
    n_exp = w_router_t.shape[0]
    tm = 512
    tok = lambda i: (0, i)
    return pl.pallas_call(
        functools.partial(_router_kernel, n_exp=n_exp),
        out_shape=(jax.ShapeDtypeStruct((TOP_K, n), jnp.int32), jax.ShapeDtypeStruct((TOP_K, n), F32),
                   jax.ShapeDtypeStruct((TOP_K, n), jnp.int32), jax.ShapeDtypeStruct((n_exp, 1), F32)),
        grid=(n // tm,),
        in_specs=[pl.BlockSpec((tm, d), lambda i: (i, 0)),
                  pl.BlockSpec((n_exp, d), lambda i: (0, 0)),
                  pl.BlockSpec((n_exp, 1), lambda i: (0, 0))],
        out_specs=(pl.BlockSpec((TOP_K, tm), tok), pl.BlockSpec((TOP_K, tm), tok), pl.BlockSpec((TOP_K, tm), tok),
                   pl.BlockSpec((n_exp, 1), lambda i: (0, 0))),
        scratch_shapes=[pltpu.VMEM((n_exp, 1), F32)],
        compiler_params=_cparams(("arbitrary",), 32),
        name="router",
    )(h2, w_router_t, b_router)


def _dest_kernel(idx_ref, rank_ref, ps_ref, o_ref, *, n_exp):
    tm = idx_ref.shape[1]
    rowf = lax.broadcasted_iota(jnp.int32, (n_exp, tm), 0).astype(F32)
    ps = ps_ref[...]
    for k in range(TOP_K):
        sel = rowf == idx_ref[k:k + 1, :].astype(F32)
        dest = jnp.sum(jnp.where(sel, ps, 0.0), axis=0, keepdims=True).astype(jnp.int32) + rank_ref[k:k + 1, :]
        for j in range(tm // LANES):
            o_ref[j, k:k + 1, :] = dest[:, j * LANES:(j + 1) * LANES]


def _dest(idx_kn, rank_kn, pstart_f):
    k, n = idx_kn.shape
    n_exp = pstart_f.shape[0]
    tm = 2048
    return pl.pallas_call(
        functools.partial(_dest_kernel, n_exp=n_exp),
        out_shape=jax.ShapeDtypeStruct((n // LANES, k, LANES), jnp.int32),
        grid=(n // tm,),
        in_specs=[pl.BlockSpec((k, tm), lambda i: (0, i)),
                  pl.BlockSpec((k, tm), lambda i: (0, i)),
                  pl.BlockSpec((n_exp, 1), lambda i: (0, 0))],
        out_specs=pl.BlockSpec((tm // LANES, k, LANES), lambda i: (i, 0, 0)),
        compiler_params=_cparams(("parallel",), 32),
        name="dest",
    )(idx_kn, rank_kn, pstart_f)


TOKEN_TILE = LANES


def _fill_unowned_rows(pad_start_ref, pad_len_ref, nu_ref, zeros, xs_hbm, zsem, tb, wait):
    n_exp = pad_start_ref.shape[0]
    nb = xs_hbm.shape[0] // tb

    def go(row, nrows):
        c = pltpu.make_async_copy(zeros.at[pl.ds(0, nrows)], xs_hbm.at[pl.ds(row, nrows)], zsem)
        c.wait() if wait else c.start()

    def per_expert(e, carry):
        row = pad_start_ref[e]
        left = pad_len_ref[e]
        singles = left & (SUBLANES - 1)
        for r in range(SUBLANES - 1):
            @pl.when(r < singles)
            def _(r=r):
                go(row + r, 1)
        row = row + singles
        chunk = SUBLANES
        while chunk < tb:
            @pl.when((left & chunk) != 0)
            def _(row=row, chunk=chunk):
                go(pl.multiple_of(row, chunk), chunk)
            row = row + (left & chunk)
            chunk *= 2
        return carry

    lax.fori_loop(0, n_exp, per_expert, 0)

    def per_tail_block(b, carry):
        go(pl.multiple_of(b * tb, tb), tb)
        return carry

    lax.fori_loop(nu_ref[0], nb, per_tail_block, 0)


def _dispatch_kernel(pad_start_ref, pad_len_ref, nu_ref, dest_ref, h_ref, xs_hbm, stage, zeros, sem, zsem, *, tb):
    j = pl.program_id(0)
    nj = pl.num_programs(0)
    slot = j % 2
    tm = h_ref.shape[0]

    def wait_slot(s):
        for _ in range(TOP_K):
            pltpu.make_async_copy(stage.at[s], xs_hbm.at[pl.ds(0, tm)], sem.at[s]).wait()

    @pl.when(j == 0)
    def _():
        zeros[...] = jnp.zeros_like(zeros)
        _fill_unowned_rows(pad_start_ref, pad_len_ref, nu_ref, zeros, xs_hbm, zsem, tb, wait=False)

    packed = h_ref[...]
    for s in range(2):
        @pl.when(slot == s)
        def _(s=s):
            @pl.when(j >= 2)
            def _():
                wait_slot(s)

            stage[s] = packed
            for i in range(tm):
                for k in range(TOP_K):
                    pltpu.make_async_copy(stage.at[s, pl.ds(i, 1)], xs_hbm.at[pl.ds(dest_ref[0, k, i], 1)],
                                          sem.at[s]).start(priority=k % 2)

    @pl.when(j == nj - 1)
    def _():
        wait_slot(slot)

    @pl.when(jnp.logical_and(j == nj - 1, nj >= 2))
    def _():
        wait_slot(1 - slot)

    @pl.when(j == nj - 1)
    def _():
        _fill_unowned_rows(pad_start_ref, pad_len_ref, nu_ref, zeros, xs_hbm, zsem, tb, wait=True)


def _dispatch(dest3, h2, pad_start, pad_len, n_used, n_rows, tb):
    n, d = h2.shape
    tm = TOKEN_TILE
    grid_spec = pltpu.PrefetchScalarGridSpec(
        num_scalar_prefetch=3,
        grid=(n // tm,),
        in_specs=[pl.BlockSpec((1, TOP_K, tm), lambda j, ps, pn, nu: (j, 0, 0), memory_space=pltpu.SMEM),
                  pl.BlockSpec((tm, d), lambda j, ps, pn, nu: (j, 0))],
        out_specs=pl.BlockSpec(memory_space=pl.ANY),
        scratch_shapes=[pltpu.VMEM((2, tm, d), F32), pltpu.VMEM((tb, d), F32),
                        pltpu.SemaphoreType.DMA((2,)), pltpu.SemaphoreType.DMA(())],
    )
    return pl.pallas_call(
        functools.partial(_dispatch_kernel, tb=tb),
        out_shape=jax.ShapeDtypeStruct((n_rows, d), F32),
        grid_spec=grid_spec,
        compiler_params=_cparams(("arbitrary",), 32),
        name="dispatch",
    )(pad_start, pad_len, n_used, dest3, h2)


WEIGHT_SLOTS = 3


def _expert_kernel(be_ref, first_ref, slot_ref, ahead_ref, prime_ref, nu_ref,
                   x_ref, wg_hbm, wu_hbm, wd_hbm, o_ref, wg_buf, wu_buf, wd_buf, sem):
    b = pl.program_id(0)
    nu = nu_ref[0]
    slot = slot_ref[b]

    def weight_copies(e, s):
        return (pltpu.make_async_copy(wg_hbm.at[e], wg_buf.at[s], sem.at[s, 0]),
                pltpu.make_async_copy(wu_hbm.at[e], wu_buf.at[s], sem.at[s, 1]),
                pltpu.make_async_copy(wd_hbm.at[e], wd_buf.at[s], sem.at[s, 2]))

    def start_weights(e, s):
        for i, c in enumerate(weight_copies(e, s)):
            c.start(priority=min(i, 1))

    @pl.when(b == 0)
    def _():
        for s in range(WEIGHT_SLOTS - 1):
            @pl.when(prime_ref[s] >= 0)
            def _(s=s):
                start_weights(prime_ref[s], s)

    @pl.when(first_ref[b] == 1)
    def _():
        for c in weight_copies(be_ref[b], slot):
            c.wait()
        nxt = ahead_ref[b]

        @pl.when(nxt >= 0)
        def _():
            start_weights(nxt, (slot + WEIGHT_SLOTS - 1) % WEIGHT_SLOTS)

    @pl.when(b < nu)
    def _():
        x = x_ref[...].astype(BF16)
        g = jnp.dot(x, wg_buf[slot].astype(BF16), preferred_element_type=F32)
        u = jnp.dot(x, wu_buf[slot].astype(BF16), preferred_element_type=F32)
        a = (g * _sigmoid(g)) * u
        o_ref[...] = jnp.dot(a.astype(BF16), wd_buf[slot].astype(BF16), preferred_element_type=F32)

    @pl.when(b >= nu)
    def _():
        o_ref[...] = jnp.zeros_like(o_ref)


def _experts(xs, blocks, w_eg, w_eu, w_ed, tb):
    p, dh = xs.shape
    nb = p // tb
    n_exp, d, f = w_eg.shape
    last_used = lambda b, nu: jnp.minimum(b, nu[0] - 1)
    grid_spec = pltpu.PrefetchScalarGridSpec(
        num_scalar_prefetch=6,
        grid=(nb,),
        in_specs=[pl.BlockSpec((tb, dh), lambda b, be, fi, sl, ah, pr, nu: (last_used(b, nu), 0)),
                  pl.BlockSpec(memory_space=pl.ANY),
                  pl.BlockSpec(memory_space=pl.ANY),
                  pl.BlockSpec(memory_space=pl.ANY)],
        out_specs=pl.BlockSpec((tb, dh), lambda b, be, fi, sl, ah, pr, nu: (b, 0)),
        scratch_shapes=[pltpu.VMEM((WEIGHT_SLOTS, d, f), F32), pltpu.VMEM((WEIGHT_SLOTS, d, f), F32),
                        pltpu.VMEM((WEIGHT_SLOTS, f, d), F32), pltpu.SemaphoreType.DMA((WEIGHT_SLOTS, 3))],
    )
    return pl.pallas_call(
        _expert_kernel,
        out_shape=jax.ShapeDtypeStruct((p, dh), F32),
        grid_spec=grid_spec,
        compiler_params=_cparams(("arbitrary",), 58),
        name="experts",
    )(*blocks, xs, w_eg, w_eu, w_ed)


def _shared_kernel(h_ref, wg_ref, wu_ref, wd_ref, o_ref):
    x = h_ref[...].astype(BF16)
    g = jnp.dot(x, wg_ref[...], preferred_element_type=F32)
    u = jnp.dot(x, wu_ref[...], preferred_element_type=F32)
    a = (g * _sigmoid(g)) * u
    o_ref[...] = jnp.dot(a.astype(BF16), wd_ref[...], preferred_element_type=F32)


def _shared(h2, w_sg, w_su, w_sd):
    n, d = h2.shape
    f = w_sg.shape[1]
    tm = 512
    return pl.pallas_call(
        _shared_kernel,
        out_shape=jax.ShapeDtypeStruct((n, d), F32),
        grid=(n // tm,),
        in_specs=[pl.BlockSpec((tm, d), lambda i: (i, 0)),
                  pl.BlockSpec((d, f), lambda i: (0, 0)),
                  pl.BlockSpec((d, f), lambda i: (0, 0)),
                  pl.BlockSpec((f, d), lambda i: (0, 0))],
        out_specs=pl.BlockSpec((tm, d), lambda i: (i, 0)),
        compiler_params=_cparams(("parallel",), 40),
        name="shared",
    )(h2, w_sg, w_su, w_sd)


def _gather_sorted_rows(dest_ref, ys_hbm, dst, sem):
    tm = dest_ref.shape[2]
    for k in range(TOP_K):
        for i in range(tm):
            pltpu.make_async_copy(ys_hbm.at[pl.ds(dest_ref[0, k, i], 1)], dst.at[pl.ds(k * tm + i, 1)],
                                  sem).start(priority=i % 2)


def _combine_kernel(dstc_ref, dstn_ref, ys_hbm, w_ref, sh_ref, x1_ref, g2_ref, gf_ref, o_ref, buf, sem):
    j = pl.program_id(0)
    nj = pl.num_programs(0)
    slot = j % 2
    tm, d = o_ref.shape

    @pl.when(j == 0)
    def _():
        _gather_sorted_rows(dstc_ref, ys_hbm, buf.at[0], sem.at[0])

    for s in range(2):
        @pl.when(jnp.logical_and(j + 1 < nj, slot == 1 - s))
        def _(s=s):
            _gather_sorted_rows(dstn_ref, ys_hbm, buf.at[s], sem.at[s])

    pltpu.make_async_copy(ys_hbm.at[pl.ds(0, TOP_K * tm)], buf.at[slot], sem.at[slot]).wait()
    w = w_ref[...]
    gf = gf_ref[...]
    g2 = g2_ref[0]
    rows = 32
    for r in range(0, tm, rows):
        acc = w[r:r + rows, 0:1] * buf[slot, r:r + rows]
        for k in range(1, TOP_K):
            acc = acc + w[r:r + rows, k:k + 1] * buf[slot, k * tm + r:k * tm + r + rows]
        x2 = x1_ref[r:r + rows, :] + g2 * (acc + sh_ref[r:r + rows, :])
        ms = jnp.mean(x2 * x2, axis=-1, keepdims=True)
        o_ref[r:r + rows, :] = (x2 * lax.rsqrt(ms + EPS)) * gf


def _combine(dest3, ys, w_tok, shared, x1, ga2, g_final, seq):
    n, d = x1.shape
    tm = TOKEN_TILE
    nj = n // tm
    smem = lambda f: pl.BlockSpec((1, TOP_K, tm), f, memory_space=pltpu.SMEM)
    return pl.pallas_call(
        _combine_kernel,
        out_shape=jax.ShapeDtypeStruct((n, d), F32),
        grid=(nj,),
        in_specs=[smem(lambda j: (j, 0, 0)),
                  smem(lambda j: (jnp.minimum(j + 1, nj - 1), 0, 0)),
                  pl.BlockSpec(memory_space=pl.ANY),
                  pl.BlockSpec((tm, TOP_K), lambda j: (j, 0)),
                  pl.BlockSpec((tm, d), lambda j: (j, 0)),
                  pl.BlockSpec((tm, d), lambda j: (j, 0)),
                  pl.BlockSpec((1, 1, d), lambda j: ((j * tm) // seq, 0, 0)),
                  pl.BlockSpec((1, d), lambda j: (0, 0))],
        out_specs=pl.BlockSpec((tm, d), lambda j: (j, 0)),
        scratch_shapes=[pltpu.VMEM((2, TOP_K * tm, d), F32), pltpu.SemaphoreType.DMA((2,))],
        compiler_params=_cparams(("arbitrary",), 40),
        name="combine",
    )(dest3, dest3, ys, w_tok, shared, x1, ga2, g_final)


def _block_tables(counts, tb, nb):
    n_exp = counts.shape[0]
    padded = (counts + tb - 1) // tb * tb
    pend = jnp.cumsum(padded)
    pstart = pend - padded
    blk_row = jnp.arange(nb, dtype=jnp.int32) * tb
    be = jnp.minimum(jnp.sum((pend[None, :] <= blk_row[:, None]).astype(jnp.int32), axis=1), n_exp - 1)
    n_used = pend[-1] // tb
    nonempty = counts > 0
    ordinal = jnp.cumsum(nonempty.astype(jnp.int32)) - 1
    by_ord = jnp.sum(jnp.where((ordinal[None, :] == jnp.arange(n_exp + WEIGHT_SLOTS)[:, None]) & nonempty[None, :],
                               jnp.arange(n_exp, dtype=jnp.int32)[None, :] - n_exp, 0), axis=1) + n_exp
    sel = (be[:, None] == jnp.arange(n_exp, dtype=jnp.int32)[None, :]).astype(jnp.int32)
    pick = lambda v: jnp.sum(sel * v[None, :], axis=1)
    b_start, b_ord = pick(pstart), pick(ordinal)
    ord_sel = (b_ord[:, None] + (WEIGHT_SLOTS - 1) == jnp.arange(n_exp + WEIGHT_SLOTS)[None, :]).astype(jnp.int32)
    b_ahead = jnp.sum(ord_sel * by_ord[None, :], axis=1)
    used = jnp.arange(nb) < n_used
    first = jnp.logical_and(blk_row == b_start, used).astype(jnp.int32)
    b_ahead = jnp.where(b_ahead < n_exp, b_ahead, -1)
    prime = jnp.where(by_ord[:WEIGHT_SLOTS - 1] < n_exp, by_ord[:WEIGHT_SLOTS - 1], -1)
    i32 = lambda v: v.astype(jnp.int32)
    blocks = (i32(be), first, i32(b_ord % WEIGHT_SLOTS), i32(b_ahead), i32(prime), i32(n_used).reshape(1))
    pads = (i32(pstart + counts), i32(padded - counts), i32(n_used).reshape(1))
    return pstart, blocks, pads


def kernel(x, c, ctx, c_ctx, w_ada, b_ada, g_mix, w_in, na_rpb, w_na, w_four, w_out, g_ffn, w_router, b_router,
           w_exp_gate, w_exp_up, w_exp_down, w_sh_gate, w_sh_up, w_sh_down, g_final):
    depth = w_ada.shape[0]
    assert depth == 1, "single-layer kernel: the context stream is never updated"
    b, s, d = x.shape
    n = b * s
    n_exp = w_router.shape[-1]
    rows = s // GRID_W
    assert s % (GRID_W * NA_ROWS_PER_STEP) == 0 and rows >= NA_KH and c.shape[0] + 1 <= 8

    c8 = jnp.zeros((8, d), F32).at[:b].set(c).at[b].set(c_ctx)
    mod = _ada(c8, w_ada[0], b_ada[0][None])
    sh1, sc1, ga1, sh2, sc2, ga2 = [m[:b, None, :] for m in jnp.split(mod, 6, axis=-1)]
    csh1, csc1 = [jnp.broadcast_to(m[b][None, None, :], (b, 1, d)) for m in jnp.split(mod, 6, axis=-1)[:2]]

    w_in_b = w_in[0].astype(BF16)
    g_mix2 = g_mix[0][None]
    proj = _proj(x, g_mix2, sc1, sh1, w_in_b, 0, w_in_b.shape[1], 1024, 1024)
    kvx = _proj(ctx, g_mix2, csc1, csh1, w_in_b, D_NA, 2 * D_NA, ctx.shape[1], 512)

    o_na = _na_attention(proj, kvx, _na_bias_table(na_rpb[0]))
    four = _fourier(proj[:, :, 3 * D_NA:3 * D_NA + D_FOURIER])

    x1, h2 = _merge(o_na, four, proj, x, ga1, sc2, sh2, g_ffn[0][None],
                    w_na[0].astype(BF16), w_four[0].astype(BF16), w_out[0].astype(BF16))
    x1 = x1.reshape(n, d)
    h2 = h2.reshape(n, d)

    idx_kn, w_kn, rank_kn, counts = _router(h2, w_router[0].T, b_router[0][:, None])
    nb = (n * TOP_K + n_exp * (EXPERT_ROWS - 1)) // EXPERT_ROWS
    pstart, blocks, pads = _block_tables(counts[:, 0].astype(jnp.int32), EXPERT_ROWS, nb)
    dest3 = _dest(idx_kn, rank_kn, pstart.astype(F32)[:, None])
    xs = _dispatch(dest3, h2, *pads, nb * EXPERT_ROWS, EXPERT_ROWS)
    ys = _experts(xs, blocks, w_exp_gate[0], w_exp_up[0], w_exp_down[0], EXPERT_ROWS)
    shared = _shared(h2, w_sh_gate[0].astype(BF16), w_sh_up[0].astype(BF16), w_sh_down[0].astype(BF16))
    out = _combine(dest3, ys, w_kn.T, shared, x1, ga2, g_final[None], s)
    return out.reshape(b, s, d)
```

```python
import functools

import numpy as np
import jax
import jax.numpy as jnp
from jax import lax
from jax.experimental import pallas as pl
from jax.experimental.pallas import tpu as pltpu

F32 = jnp.float32
BF16 = jnp.bfloat16

GRID_W = 64
N_HEADS = 16
HEAD_DIM = 64
D_NA = N_HEADS * HEAD_DIM
NA_KH = 8
NA_KW = 16
F_GROUPS = 8
F_GROUP_DIM = 128
D_FOURIER = F_GROUPS * F_GROUP_DIM
N_EXPERT_GROUPS = 8
TOPK_GROUPS = 4
TOP_K = 8
ROUTED_SCALE = 2.5
EPS = 1e-6
NEG_MASK = -1e30

LANES = 128
SUBLANES = 8
EXPERT_ROWS = 256
MIB = 1024 * 1024


def _cparams(sem, vmem_mib):
    return pltpu.CompilerParams(dimension_semantics=sem, vmem_limit_bytes=vmem_mib * MIB)


def _sigmoid(v):
    return 1.0 / (1.0 + jnp.exp(-v))


def _split_bf16(a):
    hi = a.astype(BF16)
    lo = (a - hi.astype(F32)).astype(BF16)
    return hi, lo


def _dot3(a, b, dims):
    ah, al = _split_bf16(a)
    bh, bl = _split_bf16(b)
    d = lambda p, q: lax.dot_general(p, q, dims, preferred_element_type=F32)
    return d(ah, bh) + (d(ah, bl) + d(al, bh))


_NN = (((1,), (0,)), ((), ()))
_NT = (((1,), (1,)), ((), ()))


def _ada_kernel(c_ref, w_ref, b_ref, o_ref):
    c = c_ref[...]
    a = c * _sigmoid(c)
    o_ref[...] = _dot3(a, w_ref[...], _NN) + b_ref[...]


def _ada(c8, w_ada, b_ada):
    d, n = w_ada.shape
    tn = 512
    return pl.pallas_call(
        _ada_kernel,
        out_shape=jax.ShapeDtypeStruct((8, n), F32),
        grid=(n // tn,),
        in_specs=[pl.BlockSpec((8, d), lambda j: (0, 0)),
                  pl.BlockSpec((d, tn), lambda j: (0, j)),
                  pl.BlockSpec((1, tn), lambda j: (0, j))],
        out_specs=pl.BlockSpec((8, tn), lambda j: (0, j)),
        compiler_params=_cparams(("parallel",), 48),
        name="ada",
    )(c8, w_ada, b_ada)


def _proj_kernel(x_ref, g_ref, sc_ref, sh_ref, w_ref, o_ref, h_ref):
    @pl.when(pl.program_id(2) == 0)
    def _():
        x = x_ref[0]
        ms = jnp.mean(x * x, axis=-1, keepdims=True)
        y = (x * lax.rsqrt(ms + EPS)) * g_ref[...]
        h_ref[...] = (y * (1.0 + sc_ref[0]) + sh_ref[0]).astype(BF16)

    o_ref[0] = jnp.dot(h_ref[...], w_ref[...], preferred_element_type=F32).astype(o_ref.dtype)


def _proj(x, g, sc, sh, w, col0, ncols, tm, tn):
    b, s, d = x.shape
    j0 = col0 // tn
    return pl.pallas_call(
        _proj_kernel,
        out_shape=jax.ShapeDtypeStruct((b, s, ncols), BF16),
        grid=(b, s // tm, ncols // tn),
        in_specs=[pl.BlockSpec((1, tm, d), lambda bi, i, j: (bi, i, 0)),
                  pl.BlockSpec((1, d), lambda bi, i, j: (0, 0)),
                  pl.BlockSpec((1, 1, d), lambda bi, i, j: (bi, 0, 0)),
                  pl.BlockSpec((1, 1, d), lambda bi, i, j: (bi, 0, 0)),
                  pl.BlockSpec((d, tn), lambda bi, i, j: (0, j + j0))],
        out_specs=pl.BlockSpec((1, tm, tn), lambda bi, i, j: (bi, i, j)),
        scratch_shapes=[pltpu.VMEM((tm, d), BF16)],
        compiler_params=_cparams(("parallel", "parallel", "arbitrary"), 48),
        name="proj",
    )(x, g, sc, sh, w)


NA_ROWS_PER_STEP = 8
NA_TOK = NA_ROWS_PER_STEP * GRID_W


def _na_bias_table(rpb):
    q = np.arange(GRID_W)
    kc = np.arange(GRID_W)
    ws = np.clip(q - NA_KW // 2, 0, GRID_W - NA_KW)
    inwin = (kc[None, :] >= ws[:, None]) & (kc[None, :] < ws[:, None] + NA_KW)
    ext = jnp.pad(rpb.astype(F32), ((0, 0), (0, 0), (GRID_W - NA_KW, GRID_W - NA_KW)), mode="edge")
    c = jnp.stack([ext[:, :, GRID_W - 1 - qi:2 * GRID_W - 1 - qi] for qi in range(GRID_W)], axis=2)
    c = jnp.where(jnp.asarray(inwin)[None, None], c, NEG_MASK)
    c2 = jnp.concatenate([c[:, :-1], c[:, 1:]], axis=-1)
    nd = c2.shape[1]
    c2 = c2.reshape(N_HEADS // 2, 2, nd, GRID_W, 2 * GRID_W).transpose(0, 2, 1, 3, 4)
    return c2.reshape(N_HEADS // 2, nd, 2 * GRID_W, 2 * GRID_W)


def _na_kernel(q_ref, kp_ref, kc_ref, kn_ref, vp_ref, vc_ref, vn_ref, kx_ref, vx_ref, bias_ref,
               o_ref, kbuf, vbuf, q_scr, s_scr, p_scr, *, rows):
    t = pl.program_id(2)
    kbuf[0:NA_TOK] = kp_ref[0]
    kbuf[NA_TOK:2 * NA_TOK] = kc_ref[0]
    kbuf[2 * NA_TOK:3 * NA_TOK] = kn_ref[0]
    vbuf[0:NA_TOK] = vp_ref[0]
    vbuf[NA_TOK:2 * NA_TOK] = vc_ref[0]
    vbuf[2 * NA_TOK:3 * NA_TOK] = vn_ref[0]
    even = lax.broadcasted_iota(jnp.int32, (GRID_W, LANES), 1) < HEAD_DIM
    kx = kx_ref[0]
    vx = vx_ref[0]
    nwin = NA_KH * GRID_W
    qrows = 2 * GRID_W

    offs = []
    for i in range(NA_ROWS_PER_STEP):
        q = q_ref[0, i * GRID_W:(i + 1) * GRID_W, :]
        zero = jnp.zeros_like(q)
        q_scr[i * qrows:i * qrows + GRID_W] = jnp.where(even, q, zero) * 0.125
        q_scr[i * qrows + GRID_W:(i + 1) * qrows] = jnp.where(even, zero, q) * 0.125
    s_scr[:, nwin:] = lax.dot_general(q_scr[...], kx, _NT, preferred_element_type=F32)
    for i in range(NA_ROWS_PER_STEP):
        r = t * NA_ROWS_PER_STEP + i
        rs = jnp.clip(r - NA_KH // 2, 0, rows - NA_KH)
        off = pl.multiple_of((rs - (t - 1) * NA_ROWS_PER_STEP) * GRID_W, GRID_W)
        offs.append(off)
        d0 = rs - r + NA_KH - 1
        s_lat = lax.dot_general(q_scr[i * qrows:(i + 1) * qrows], kbuf[pl.ds(off, nwin), :], _NT,
                                preferred_element_type=F32)
        bias = jnp.concatenate([bias_ref[0, d0 + 2 * j] for j in range(NA_KH // 2)], axis=1)
        s_scr[i * qrows:(i + 1) * qrows, 0:nwin] = s_lat + bias
    inv_l = []
    for i in range(NA_ROWS_PER_STEP):
        s = s_scr[i * qrows:(i + 1) * qrows, :]
        p = jnp.exp(s - jnp.max(s, axis=1, keepdims=True))
        inv_l.append(1.0 / jnp.sum(p, axis=1, keepdims=True))
        p_scr[i * qrows:(i + 1) * qrows, :] = p.astype(BF16)
    o_ctx = jnp.dot(p_scr[:, nwin:], vx, preferred_element_type=F32)
    for i in range(NA_ROWS_PER_STEP):
        o2 = jnp.dot(p_scr[i * qrows:(i + 1) * qrows, 0:nwin], vbuf[pl.ds(offs[i], nwin), :],
                     preferred_element_type=F32)
        o2 = (o2 + o_ctx[i * qrows:(i + 1) * qrows]) * inv_l[i]
        o = jnp.where(even, o2[0:GRID_W], o2[GRID_W:qrows])
        o_ref[0, i * GRID_W:(i + 1) * GRID_W, :] = o.astype(o_ref.dtype)


def _na_attention(proj, kvx, bias):
    b, s, _ = proj.shape
    lctx = kvx.shape[1]
    rows = s // GRID_W
    ngrp = rows // NA_ROWS_PER_STEP
    npair = N_HEADS // 2
    kcol, vcol = D_NA // LANES, 2 * D_NA // LANES
    blk = (1, NA_TOK, LANES)
    prev = lambda t: jnp.maximum(t - 1, 0)
    nxt = lambda t: jnp.minimum(t + 1, ngrp - 1)
    return pl.pallas_call(
        functools.partial(_na_kernel, rows=rows),
        out_shape=jax.ShapeDtypeStruct((b, s, D_NA), BF16),
        grid=(b, npair, ngrp),
        in_specs=[pl.BlockSpec(blk, lambda bi, p, t: (bi, t, p)),
                  pl.BlockSpec(blk, lambda bi, p, t: (bi, prev(t), kcol + p)),
                  pl.BlockSpec(blk, lambda bi, p, t: (bi, t, kcol + p)),
                  pl.BlockSpec(blk, lambda bi, p, t: (bi, nxt(t), kcol + p)),
                  pl.BlockSpec(blk, lambda bi, p, t: (bi, prev(t), vcol + p)),
                  pl.BlockSpec(blk, lambda bi, p, t: (bi, t, vcol + p)),
                  pl.BlockSpec(blk, lambda bi, p, t: (bi, nxt(t), vcol + p)),
                  pl.BlockSpec((1, lctx, LANES), lambda bi, p, t: (bi, 0, p)),
                  pl.BlockSpec((1, lctx, LANES), lambda bi, p, t: (bi, 0, kcol + p)),
                  pl.BlockSpec((1,) + bias.shape[1:], lambda bi, p, t: (p, 0, 0, 0))],
        out_specs=pl.BlockSpec(blk, lambda bi, p, t: (bi, t, p)),
        scratch_shapes=[pltpu.VMEM((3 * NA_TOK, LANES), BF16), pltpu.VMEM((3 * NA_TOK, LANES), BF16),
                        pltpu.VMEM((2 * NA_TOK, LANES), BF16),
                        pltpu.VMEM((2 * NA_TOK, NA_KH * GRID_W + lctx), F32),
                        pltpu.VMEM((2 * NA_TOK, NA_KH * GRID_W + lctx), BF16)],
        compiler_params=_cparams(("parallel", "parallel", "arbitrary"), 32),
        name="na",
    )(proj, proj, proj, proj, proj, proj, proj, kvx, kvx, bias)


def _dft_consts(n1, n2):
    n = n1 * n2
    a1 = 2.0 * np.pi * np.outer(np.arange(n1), np.arange(n1)) / n1
    fa = np.concatenate([np.cos(a1), -np.sin(a1)], axis=0)
    tw = 2.0 * np.pi * np.outer(np.arange(n2), np.arange(n1)) / n
    twc = np.repeat(np.cos(tw)[:, :, None], LANES, axis=2)
    tws = np.repeat(np.sin(tw)[:, :, None], LANES, axis=2)
    a2 = 2.0 * np.pi * np.outer(np.arange(n2), np.arange(n2)) / n2
    c2, s2 = np.cos(a2), np.sin(a2)
    fb = np.block([[c2, s2], [-s2, c2]])
    ac = 2.0 * np.pi * np.outer(np.arange(F_GROUP_DIM), np.arange(F_GROUP_DIM)) / F_GROUP_DIM
    cs = np.concatenate([np.cos(ac), np.sin(ac)], axis=0) / np.sqrt(n * F_GROUP_DIM)
    f = lambda v: jnp.asarray(v.astype(np.float32))
    return f(fa), f(twc), f(tws), f(fb), f(cs)


DFT_STEP = 4


def _dft_a_kernel(x_ref, fa_ref, c_ref, s_ref, o_ref, *, n1):
    fa = fa_ref[...].astype(BF16)
    for j in range(DFT_STEP):
        y = jnp.dot(fa, x_ref[0, j], preferred_element_type=F32)
        c = c_ref[j]
        s = s_ref[j]
        for g in range(D_FOURIER // LANES):
            sl = slice(g * LANES, (g + 1) * LANES)
            yr, yi = y[0:n1, sl], y[n1:2 * n1, sl]
            o_ref[0, j, 0:n1, sl] = (yr * c + yi * s).astype(o_ref.dtype)
            o_ref[0, j, n1:2 * n1, sl] = (yi * c - yr * s).astype(o_ref.dtype)


def _dft_b_kernel(y_ref, fb_ref, cs_ref, o_ref, *, n2):
    fb = fb_ref[...].astype(BF16)
    cc = cs_ref[0:F_GROUP_DIM, :].astype(BF16)
    sc = cs_ref[F_GROUP_DIM:2 * F_GROUP_DIM, :].astype(BF16)
    for j in range(DFT_STEP):
        v = jnp.dot(fb, y_ref[0, j], preferred_element_type=F32)
        for g in range(F_GROUPS):
            sl = slice(g * F_GROUP_DIM, (g + 1) * F_GROUP_DIM)
            vr = v[0:n2, sl].astype(BF16)
            vi = v[n2:2 * n2, sl].astype(BF16)
            o_ref[0, j, :, sl] = (jnp.dot(vr, cc, preferred_element_type=F32)
                                  + jnp.dot(vi, sc, preferred_element_type=F32)).astype(o_ref.dtype)


def _fourier(u):
    b, s, c = u.shape
    n2 = GRID_W
    n1 = s // n2
    fa, twc, tws, fb, cs = _dft_consts(n1, n2)
    xt = u.reshape(b, n1, n2, c).transpose(0, 2, 1, 3)
    ya = pl.pallas_call(
        functools.partial(_dft_a_kernel, n1=n1),
        out_shape=jax.ShapeDtypeStruct((b, n2, 2 * n1, c), BF16),
        grid=(b, n2 // DFT_STEP),
        in_specs=[pl.BlockSpec((1, DFT_STEP, n1, c), lambda bi, j: (bi, j, 0, 0)),
                  pl.BlockSpec((2 * n1, n1), lambda bi, j: (0, 0)),
                  pl.BlockSpec((DFT_STEP, n1, LANES), lambda bi, j: (j, 0, 0)),
                  pl.BlockSpec((DFT_STEP, n1, LANES), lambda bi, j: (j, 0, 0))],
        out_specs=pl.BlockSpec((1, DFT_STEP, 2 * n1, c), lambda bi, j: (bi, j, 0, 0)),
        compiler_params=_cparams(("parallel", "parallel"), 32),
        name="dft_a",
    )(xt, fa, twc, tws)
    yt = ya.reshape(b, n2, 2, n1, c).transpose(0, 3, 2, 1, 4).reshape(b, n1, 2 * n2, c)
    vb = pl.pallas_call(
        functools.partial(_dft_b_kernel, n2=n2),
        out_shape=jax.ShapeDtypeStruct((b, n1, n2, c), BF16),
        grid=(b, n1 // DFT_STEP),
        in_specs=[pl.BlockSpec((1, DFT_STEP, 2 * n2, c), lambda bi, j: (bi, j, 0, 0)),
                  pl.BlockSpec((2 * n2, 2 * n2), lambda bi, j: (0, 0)),
                  pl.BlockSpec((2 * F_GROUP_DIM, F_GROUP_DIM), lambda bi, j: (0, 0))],
        out_specs=pl.BlockSpec((1, DFT_STEP, n2, c), lambda bi, j: (bi, j, 0, 0)),
        compiler_params=_cparams(("parallel", "parallel"), 32),
        name="dft_b",
    )(yt, fb, cs)
    return vb.transpose(0, 2, 1, 3).reshape(b, s, c)


def _merge_kernel(o_ref, f_ref, ga_ref, gb_ref, x_ref, g1_ref, sc_ref, sh_ref, gf_ref,
                  wna_ref, wf_ref, wo_ref, x1_ref, h2_ref):
    ya = jnp.dot(o_ref[0], wna_ref[...], preferred_element_type=F32)
    yf = jnp.dot(f_ref[0], wf_ref[...], preferred_element_type=F32)
    m = _sigmoid(ga_ref[0].astype(F32)) * ya + _sigmoid(gb_ref[0].astype(F32)) * yf
    out = jnp.dot(m.astype(BF16), wo_ref[...], preferred_element_type=F32)
    x1 = x_ref[0] + g1_ref[0] * out
    x1_ref[0] = x1
    ms = jnp.mean(x1 * x1, axis=-1, keepdims=True)
    y = (x1 * lax.rsqrt(ms + EPS)) * gf_ref[...]
    h2_ref[0] = y * (1.0 + sc_ref[0]) + sh_ref[0]


def _merge(o_na, four, proj, x, ga1, sc2, sh2, g_ffn, w_na, w_four, w_out):
    b, s, d = x.shape
    tm = 256
    ga_blk = (D_NA * 3 + D_FOURIER) // d
    row = lambda bi, i: (bi, i, 0)
    per_b = lambda bi, i: (bi, 0, 0)
    full = lambda bi, i: (0, 0)
    return pl.pallas_call(
        _merge_kernel,
        out_shape=(jax.ShapeDtypeStruct((b, s, d), F32), jax.ShapeDtypeStruct((b, s, d), F32)),
        grid=(b, s // tm),
        in_specs=[pl.BlockSpec((1, tm, D_NA), row),
                  pl.BlockSpec((1, tm, D_FOURIER), row),
                  pl.BlockSpec((1, tm, d), lambda bi, i: (bi, i, ga_blk)),
                  pl.BlockSpec((1, tm, d), lambda bi, i: (bi, i, ga_blk + 1)),
                  pl.BlockSpec((1, tm, d), row),
                  pl.BlockSpec((1, 1, d), per_b),
                  pl.BlockSpec((1, 1, d), per_b),
                  pl.BlockSpec((1, 1, d), per_b),
                  pl.BlockSpec((1, d), full),
                  pl.BlockSpec((D_NA, d), full),
                  pl.BlockSpec((D_FOURIER, d), full),
                  pl.BlockSpec((d, d), full)],
        out_specs=(pl.BlockSpec((1, tm, d), row), pl.BlockSpec((1, tm, d), row)),
        compiler_params=_cparams(("parallel", "parallel"), 56),
        name="merge",
    )(o_na, four, proj, proj, x, ga1, sc2, sh2, g_ffn, w_na, w_four, w_out)


def _router_kernel(h_ref, wt_ref, b_ref, idx_ref, w_ref, rank_ref, cnt_ref, base_scr, *, n_exp):
    tm = h_ref.shape[0]

    @pl.when(pl.program_id(0) == 0)
    def _():
        base_scr[...] = jnp.zeros_like(base_scr)

    gsz = n_exp // N_EXPERT_GROUPS
    s = _sigmoid(_dot3(wt_ref[...], h_ref[...], _NT))
    ssel = s + b_ref[...]
    rowf = lax.broadcasted_iota(jnp.int32, (n_exp, tm), 0).astype(F32)
    big = float(n_exp)
    neg = -jnp.inf
    cmax = lambda v: jnp.max(v, axis=0, keepdims=True)
    cmin = lambda v: jnp.min(v, axis=0, keepdims=True)

    gs = []
    for g in range(N_EXPERT_GROUPS):
        xg = ssel[g * gsz:(g + 1) * gsz]
        ig = (lax.broadcasted_iota(jnp.int32, (gsz, tm), 0) + g * gsz).astype(F32)
        m1 = cmax(xg)
        i1 = cmin(jnp.where(xg == m1, ig, big))
        m2 = cmax(jnp.where(ig == i1, neg, xg))
        gs.append(m1 + m2)
    parts = []
    for g in range(N_EXPERT_GROUPS):
        beaten = jnp.zeros((1, tm), F32)
        for o in range(N_EXPERT_GROUPS):
            if o != g:
                wins = (gs[o] >= gs[g]) if o < g else (gs[o] > gs[g])
                beaten = beaten + jnp.where(wins, 1.0, 0.0)
        keep = beaten < float(TOPK_GROUPS)
        parts.append(jnp.where(keep, ssel[g * gsz:(g + 1) * gsz], neg))
    cand = jnp.concatenate(parts, axis=0)
    ws, picks = [], []
    for k in range(TOP_K):
        m = cmax(cand)
        i = cmin(jnp.where(cand == m, rowf, big))
        hit = rowf == i
        ws.append(jnp.sum(jnp.where(hit, s, 0.0), axis=0, keepdims=True))
        picks.append(i)
        idx_ref[k:k + 1, :] = i.astype(jnp.int32)
        cand = jnp.where(hit, neg, cand)
    den = ws[0]
    for k in range(1, TOP_K):
        den = den + ws[k]
    for k in range(TOP_K):
        w_ref[k:k + 1, :] = ws[k] / den * ROUTED_SCALE

    hits = jnp.zeros((n_exp, tm), F32)
    for k in range(TOP_K):
        hits = hits + jnp.where(rowf == picks[k], 1.0, 0.0)
    earlier = (lax.broadcasted_iota(jnp.int32, (tm, tm), 0) < lax.broadcasted_iota(jnp.int32, (tm, tm), 1))
    before = jnp.dot(hits.astype(BF16), jnp.where(earlier, 1.0, 0.0).astype(BF16),
                     preferred_element_type=F32) + base_scr[...]
    for k in range(TOP_K):
        sel = rowf == picks[k]
        rank_ref[k:k + 1, :] = jnp.sum(jnp.where(sel, before, 0.0), axis=0, keepdims=True).astype(jnp.int32)
    base_scr[...] = base_scr[...] + jnp.sum(hits, axis=1, keepdims=True)
    cnt_ref[...] = base_scr[...]


def _router(h2, w_router_t, b_router):
    n, d = h2.shape
    n_exp = w_router_t.shape[0]
    tm = 512
    tok = lambda i: (0, i)
    return pl.pallas_call(
        functools.partial(_router_kernel, n_exp=n_exp),
        out_shape=(jax.ShapeDtypeStruct((TOP_K, n), jnp.int32), jax.ShapeDtypeStruct((TOP_K, n), F32),
                   jax.ShapeDtypeStruct((TOP_K, n), jnp.int32), jax.ShapeDtypeStruct((n_exp, 1), F32)),
        grid=(n // tm,),
        in_specs=[pl.BlockSpec((tm, d), lambda i: (i, 0)),
                  pl.BlockSpec((n_exp, d), lambda i: (0, 0)),
                  pl.BlockSpec((n_exp, 1), lambda i: (0, 0))],
        out_specs=(pl.BlockSpec((TOP_K, tm), tok), pl.BlockSpec((TOP_K, tm), tok), pl.BlockSpec((TOP_K, tm), tok),
                   pl.BlockSpec((n_exp, 1), lambda i: (0, 0))),
        scratch_shapes=[pltpu.VMEM((n_exp, 1), F32)],
        compiler_params=_cparams(("arbitrary",), 32),
        name="router",
    )(h2, w_router_t, b_router)


def _dest_kernel(idx_ref, rank_ref, ps_ref, o_ref, *, n_exp):
    tm = idx_ref.shape[1]
    rowf = lax.broadcasted_iota(jnp.int32, (n_exp, tm), 0).astype(F32)
    ps = ps_ref[...]
    for k in range(TOP_K):
        sel = rowf == idx_ref[k:k + 1, :].astype(F32)
        dest = jnp.sum(jnp.where(sel, ps, 0.0), axis=0, keepdims=True).astype(jnp.int32) + rank_ref[k:k + 1, :]
        for j in range(tm // LANES):
            o_ref[j, k:k + 1, :] = dest[:, j * LANES:(j + 1) * LANES]


def _dest(idx_kn, rank_kn, pstart_f):
    k, n = idx_kn.shape
    n_exp = pstart_f.shape[0]
    tm = 2048
    return pl.pallas_call(
        functools.partial(_dest_kernel, n_exp=n_exp),
        out_shape=jax.ShapeDtypeStruct((n // LANES, k, LANES), jnp.int32),
        grid=(n // tm,),
        in_specs=[pl.BlockSpec((k, tm), lambda i: (0, i)),
                  pl.BlockSpec((k, tm), lambda i: (0, i)),
                  pl.BlockSpec((n_exp, 1), lambda i: (0, 0))],
        out_specs=pl.BlockSpec((tm // LANES, k, LANES), lambda i: (i, 0, 0)),
        compiler_params=_cparams(("parallel",), 32),
        name="dest",
    )(idx_kn, rank_kn, pstart_f)


TOKEN_TILE = LANES


def _to_row_tiles(x):
    return x.reshape(x.shape[0], x.shape[1] // LANES, LANES)


def _from_row_tiles(x):
    return x.reshape(x.shape[0], x.shape[1] * x.shape[2])


def _fill_unowned_rows(pad_start_ref, pad_len_ref, nu_ref, zeros, xs_hbm, zsem, tb, wait):
    n_exp = pad_start_ref.shape[0]
    nb = xs_hbm.shape[0] // tb

    def go(row, nrows):
        c = pltpu.make_async_copy(zeros.at[pl.ds(0, nrows)], xs_hbm.at[pl.ds(row, nrows)], zsem)
        c.wait() if wait else c.start()

    def per_expert(e, carry):
        row = pad_start_ref[e]
        left = pad_len_ref[e]
        singles = left & (SUBLANES - 1)
        for r in range(SUBLANES - 1):
            @pl.when(r < singles)
            def _(r=r):
                go(row + r, 1)
        row = row + singles
        chunk = SUBLANES
        while chunk < tb:
            @pl.when((left & chunk) != 0)
            def _(row=row, chunk=chunk):
                go(pl.multiple_of(row, chunk), chunk)
            row = row + (left & chunk)
            chunk *= 2
        return carry

    lax.fori_loop(0, n_exp, per_expert, 0)

    def per_tail_block(b, carry):
        go(pl.multiple_of(b * tb, tb), tb)
        return carry

    lax.fori_loop(nu_ref[0], nb, per_tail_block, 0)


def _dispatch_kernel(pad_start_ref, pad_len_ref, nu_ref, dest_ref, h_ref, xs_hbm, stage, zeros, sem, zsem, *, tb):
    j = pl.program_id(0)
    nj = pl.num_programs(0)
    slot = j % 2
    tm = h_ref.shape[0]

    def wait_slot(s):
        for _ in range(TOP_K):
            pltpu.make_async_copy(stage.at[s], xs_hbm.at[pl.ds(0, tm)], sem.at[s]).wait()

    @pl.when(j == 0)
    def _():
        zeros[...] = jnp.zeros_like(zeros)
        _fill_unowned_rows(pad_start_ref, pad_len_ref, nu_ref, zeros, xs_hbm, zsem, tb, wait=False)

    packed = _to_row_tiles(h_ref[...].astype(BF16))
    for s in range(2):
        @pl.when(slot == s)
        def _(s=s):
            @pl.when(j >= 2)
            def _():
                wait_slot(s)

            stage[s] = packed
            for i in range(tm):
                for k in range(TOP_K):
                    pltpu.make_async_copy(stage.at[s, i], xs_hbm.at[dest_ref[0, k, i]],
                                          sem.at[s]).start(priority=k % 2)

    @pl.when(j == nj - 1)
    def _():
        wait_slot(slot)

    @pl.when(jnp.logical_and(j == nj - 1, nj >= 2))
    def _():
        wait_slot(1 - slot)

    @pl.when(j == nj - 1)
    def _():
        _fill_unowned_rows(pad_start_ref, pad_len_ref, nu_ref, zeros, xs_hbm, zsem, tb, wait=True)


def _dispatch(dest3, h2, pad_start, pad_len, n_used, n_rows, tb):
    n, d = h2.shape
    tm = TOKEN_TILE
    grid_spec = pltpu.PrefetchScalarGridSpec(
        num_scalar_prefetch=3,
        grid=(n // tm,),
        in_specs=[pl.BlockSpec((1, TOP_K, tm), lambda j, ps, pn, nu: (j, 0, 0), memory_space=pltpu.SMEM),
                  pl.BlockSpec((tm, d), lambda j, ps, pn, nu: (j, 0))],
        out_specs=pl.BlockSpec(memory_space=pl.ANY),
        scratch_shapes=[pltpu.VMEM((2, tm, d // LANES, LANES), BF16), pltpu.VMEM((tb, d // LANES, LANES), BF16),
                        pltpu.SemaphoreType.DMA((2,)), pltpu.SemaphoreType.DMA(())],
    )
    return pl.pallas_call(
        functools.partial(_dispatch_kernel, tb=tb),
        out_shape=jax.ShapeDtypeStruct((n_rows, d // LANES, LANES), BF16),
        grid_spec=grid_spec,
        compiler_params=_cparams(("arbitrary",), 32),
        name="dispatch",
    )(pad_start, pad_len, n_used, dest3, h2)


WEIGHT_SLOTS = 3


def _expert_kernel(be_ref, first_ref, slot_ref, ahead_ref, prime_ref, nu_ref,
                   x_ref, wg_hbm, wu_hbm, wd_hbm, o_ref, wg_buf, wu_buf, wd_buf, sem):
    b = pl.program_id(0)
    nu = nu_ref[0]
    slot = slot_ref[b]

    def weight_copies(e, s):
        return (pltpu.make_async_copy(wg_hbm.at[e], wg_buf.at[s], sem.at[s, 0]),
                pltpu.make_async_copy(wu_hbm.at[e], wu_buf.at[s], sem.at[s, 1]),
                pltpu.make_async_copy(wd_hbm.at[e], wd_buf.at[s], sem.at[s, 2]))

    def start_weights(e, s):
        for i, c in enumerate(weight_copies(e, s)):
            c.start(priority=min(i, 1))

    @pl.when(b == 0)
    def _():
        for s in range(WEIGHT_SLOTS - 1):
            @pl.when(prime_ref[s] >= 0)
            def _(s=s):
                start_weights(prime_ref[s], s)

    @pl.when(first_ref[b] == 1)
    def _():
        for c in weight_copies(be_ref[b], slot):
            c.wait()
        nxt = ahead_ref[b]

        @pl.when(nxt >= 0)
        def _():
            start_weights(nxt, (slot + WEIGHT_SLOTS - 1) % WEIGHT_SLOTS)

    @pl.when(b < nu)
    def _():
        x = _from_row_tiles(x_ref[...])
        g = jnp.dot(x, wg_buf[slot].astype(BF16), preferred_element_type=F32)
        u = jnp.dot(x, wu_buf[slot].astype(BF16), preferred_element_type=F32)
        a = (g * _sigmoid(g)) * u
        y = jnp.dot(a.astype(BF16), wd_buf[slot].astype(BF16), preferred_element_type=F32)
        o_ref[...] = _to_row_tiles(y.astype(BF16))

    @pl.when(b >= nu)
    def _():
        o_ref[...] = jnp.zeros_like(o_ref)


def _experts(xs, blocks, w_eg, w_eu, w_ed, tb):
    p = xs.shape[0]
    blk = (tb,) + xs.shape[1:]
    nb = p // tb
    n_exp, d, f = w_eg.shape
    last_used = lambda b, nu: jnp.minimum(b, nu[0] - 1)
    grid_spec = pltpu.PrefetchScalarGridSpec(
        num_scalar_prefetch=6,
        grid=(nb,),
        in_specs=[pl.BlockSpec(blk, lambda b, be, fi, sl, ah, pr, nu: (last_used(b, nu), 0, 0)),
                  pl.BlockSpec(memory_space=pl.ANY),
                  pl.BlockSpec(memory_space=pl.ANY),
                  pl.BlockSpec(memory_space=pl.ANY)],
        out_specs=pl.BlockSpec(blk, lambda b, be, fi, sl, ah, pr, nu: (b, 0, 0)),
        scratch_shapes=[pltpu.VMEM((WEIGHT_SLOTS, d, f), F32), pltpu.VMEM((WEIGHT_SLOTS, d, f), F32),
                        pltpu.VMEM((WEIGHT_SLOTS, f, d), F32), pltpu.SemaphoreType.DMA((WEIGHT_SLOTS, 3))],
    )
    return pl.pallas_call(
        _expert_kernel,
        out_shape=jax.ShapeDtypeStruct(xs.shape, BF16),
        grid_spec=grid_spec,
        compiler_params=_cparams(("arbitrary",), 58),
        name="experts",
    )(*blocks, xs, w_eg, w_eu, w_ed)


def _shared_kernel(h_ref, wg_ref, wu_ref, wd_ref, o_ref):
    x = h_ref[...].astype(BF16)
    g = jnp.dot(x, wg_ref[...], preferred_element_type=F32)
    u = jnp.dot(x, wu_ref[...], preferred_element_type=F32)
    a = (g * _sigmoid(g)) * u
    o_ref[...] = jnp.dot(a.astype(BF16), wd_ref[...], preferred_element_type=F32)


def _shared(h2, w_sg, w_su, w_sd):
    n, d = h2.shape
    f = w_sg.shape[1]
    tm = 512
    return pl.pallas_call(
        _shared_kernel,
        out_shape=jax.ShapeDtypeStruct((n, d), F32),
        grid=(n // tm,),
        in_specs=[pl.BlockSpec((tm, d), lambda i: (i, 0)),
                  pl.BlockSpec((d, f), lambda i: (0, 0)),
                  pl.BlockSpec((d, f), lambda i: (0, 0)),
                  pl.BlockSpec((f, d), lambda i: (0, 0))],
        out_specs=pl.BlockSpec((tm, d), lambda i: (i, 0)),
        compiler_params=_cparams(("parallel",), 40),
        name="shared",
    )(h2, w_sg, w_su, w_sd)


def _gather_sorted_rows(dest_ref, ys_hbm, dst, sem):
    tm = dest_ref.shape[2]
    for k in range(TOP_K):
        for i in range(tm):
            pltpu.make_async_copy(ys_hbm.at[dest_ref[0, k, i]], dst.at[k * tm + i],
                                  sem).start(priority=i % 2)


def _combine_kernel(dstc_ref, dstn_ref, ys_hbm, w_ref, sh_ref, x1_ref, g2_ref, gf_ref, o_ref, buf, sem):
    j = pl.program_id(0)
    nj = pl.num_programs(0)
    slot = j % 2
    tm, d = o_ref.shape

    @pl.when(j == 0)
    def _():
        _gather_sorted_rows(dstc_ref, ys_hbm, buf.at[0], sem.at[0])

    for s in range(2):
        @pl.when(jnp.logical_and(j + 1 < nj, slot == 1 - s))
        def _(s=s):
            _gather_sorted_rows(dstn_ref, ys_hbm, buf.at[s], sem.at[s])

    pltpu.make_async_copy(ys_hbm.at[pl.ds(0, TOP_K * tm)], buf.at[slot], sem.at[slot]).wait()
    w = w_ref[...]
    gf = gf_ref[...]
    g2 = g2_ref[0]
    rows = 32
    for r in range(0, tm, rows):
        row = lambda k: _from_row_tiles(buf[slot, k * tm + r:k * tm + r + rows]).astype(F32)
        acc = w[r:r + rows, 0:1] * row(0)
        for k in range(1, TOP_K):
            acc = acc + w[r:r + rows, k:k + 1] * row(k)
        x2 = x1_ref[r:r + rows, :] + g2 * (acc + sh_ref[r:r + rows, :])
        ms = jnp.mean(x2 * x2, axis=-1, keepdims=True)
        o_ref[r:r + rows, :] = (x2 * lax.rsqrt(ms + EPS)) * gf


def _combine(dest3, ys, w_tok, shared, x1, ga2, g_final, seq):
    n, d = x1.shape
    tm = TOKEN_TILE
    nj = n // tm
    smem = lambda f: pl.BlockSpec((1, TOP_K, tm), f, memory_space=pltpu.SMEM)
    return pl.pallas_call(
        _combine_kernel,
        out_shape=jax.ShapeDtypeStruct((n, d), F32),
        grid=(nj,),
        in_specs=[smem(lambda j: (j, 0, 0)),
                  smem(lambda j: (jnp.minimum(j + 1, nj - 1), 0, 0)),
                  pl.BlockSpec(memory_space=pl.ANY),
                  pl.BlockSpec((tm, TOP_K), lambda j: (j, 0)),
                  pl.BlockSpec((tm, d), lambda j: (j, 0)),
                  pl.BlockSpec((tm, d), lambda j: (j, 0)),
                  pl.BlockSpec((1, 1, d), lambda j: ((j * tm) // seq, 0, 0)),
                  pl.BlockSpec((1, d), lambda j: (0, 0))],
        out_specs=pl.BlockSpec((tm, d), lambda j: (j, 0)),
        scratch_shapes=[pltpu.VMEM((2, TOP_K * tm) + ys.shape[1:], BF16), pltpu.SemaphoreType.DMA((2,))],
        compiler_params=_cparams(("arbitrary",), 40),
        name="combine",
    )(dest3, dest3, ys, w_tok, shared, x1, ga2, g_final)


def _block_tables(counts, tb, nb):
    n_exp = counts.shape[0]
    padded = (counts + tb - 1) // tb * tb
    pend = jnp.cumsum(padded)
    pstart = pend - padded
    blk_row = jnp.arange(nb, dtype=jnp.int32) * tb
    be = jnp.minimum(jnp.sum((pend[None, :] <= blk_row[:, None]).astype(jnp.int32), axis=1), n_exp - 1)
    n_used = pend[-1] // tb
    nonempty = counts > 0
    ordinal = jnp.cumsum(nonempty.astype(jnp.int32)) - 1
    by_ord = jnp.sum(jnp.where((ordinal[None, :] == jnp.arange(n_exp + WEIGHT_SLOTS)[:, None]) & nonempty[None, :],
                               jnp.arange(n_exp, dtype=jnp.int32)[None, :] - n_exp, 0), axis=1) + n_exp
    sel = (be[:, None] == jnp.arange(n_exp, dtype=jnp.int32)[None, :]).astype(jnp.int32)
    pick = lambda v: jnp.sum(sel * v[None, :], axis=1)
    b_start, b_ord = pick(pstart), pick(ordinal)
    ord_sel = (b_ord[:, None] + (WEIGHT_SLOTS - 1) == jnp.arange(n_exp + WEIGHT_SLOTS)[None, :]).astype(jnp.int32)
    b_ahead = jnp.sum(ord_sel * by_ord[None, :], axis=1)
    used = jnp.arange(nb) < n_used
    first = jnp.logical_and(blk_row == b_start, used).astype(jnp.int32)
    b_ahead = jnp.where(b_ahead < n_exp, b_ahead, -1)
    prime = jnp.where(by_ord[:WEIGHT_SLOTS - 1] < n_exp, by_ord[:WEIGHT_SLOTS - 1], -1)
    i32 = lambda v: v.astype(jnp.int32)
    blocks = (i32(be), first, i32(b_ord % WEIGHT_SLOTS), i32(b_ahead), i32(prime), i32(n_used).reshape(1))
    pads = (i32(pstart + counts), i32(padded - counts), i32(n_used).reshape(1))
    return pstart, blocks, pads


def kernel(x, c, ctx, c_ctx, w_ada, b_ada, g_mix, w_in, na_rpb, w_na, w_four, w_out, g_ffn, w_router, b_router,
           w_exp_gate, w_exp_up, w_exp_down, w_sh_gate, w_sh_up, w_sh_down, g_final):
    depth = w_ada.shape[0]
    assert depth == 1, "single-layer kernel: the context stream is never updated"
    b, s, d = x.shape
    n = b * s
    n_exp = w_router.shape[-1]
    rows = s // GRID_W
    assert s % (GRID_W * NA_ROWS_PER_STEP) == 0 and rows >= NA_KH and c.shape[0] + 1 <= 8

    c8 = jnp.zeros((8, d), F32).at[:b].set(c).at[b].set(c_ctx)
    mod = _ada(c8, w_ada[0], b_ada[0][None])
    sh1, sc1, ga1, sh2, sc2, ga2 = [m[:b, None, :] for m in jnp.split(mod, 6, axis=-1)]
    csh1, csc1 = [jnp.broadcast_to(m[b][None, None, :], (b, 1, d)) for m in jnp.split(mod, 6, axis=-1)[:2]]

    w_in_b = w_in[0].astype(BF16)
    g_mix2 = g_mix[0][None]
    proj = _proj(x, g_mix2, sc1, sh1, w_in_b, 0, w_in_b.shape[1], 1024, 1024)
    kvx = _proj(ctx, g_mix2, csc1, csh1, w_in_b, D_NA, 2 * D_NA, ctx.shape[1], 512)

    o_na = _na_attention(proj, kvx, _na_bias_table(na_rpb[0]))
    four = _fourier(proj[:, :, 3 * D_NA:3 * D_NA + D_FOURIER])

    x1, h2 = _merge(o_na, four, proj, x, ga1, sc2, sh2, g_ffn[0][None],
                    w_na[0].astype(BF16), w_four[0].astype(BF16), w_out[0].astype(BF16))
    x1 = x1.reshape(n, d)
    h2 = h2.reshape(n, d)

    idx_kn, w_kn, rank_kn, counts = _router(h2, w_router[0].T, b_router[0][:, None])
    nb = (n * TOP_K + n_exp * (EXPERT_ROWS - 1)) // EXPERT_ROWS
    pstart, blocks, pads = _block_tables(counts[:, 0].astype(jnp.int32), EXPERT_ROWS, nb)
    dest3 = _dest(idx_kn, rank_kn, pstart.astype(F32)[:, None])
    xs = _dispatch(dest3, h2, *pads, nb * EXPERT_ROWS, EXPERT_ROWS)
    ys = _experts(xs, blocks, w_exp_gate[0], w_exp_up[0], w_exp_down[0], EXPERT_ROWS)
    shared = _shared(h2, w_sh_gate[0].astype(BF16), w_sh_up[0].astype(BF16), w_sh_down[0].astype(BF16))
    out = _combine(dest3, ys, w_kn.T, shared, x1, ga2, g_final[None], s)
    return out.reshape(b, s, d)
```

```python
import functools

import numpy as np
import jax
import jax.numpy as jnp
from jax import lax
from jax.experimental import pallas as pl
from jax.experimental.pallas import tpu as pltpu

F32 = jnp.float32
BF16 = jnp.bfloat16

GRID_W = 64
N_HEADS = 16
HEAD_DIM = 64
D_NA = N_HEADS * HEAD_DIM
NA_KH = 8
NA_KW = 16
F_GROUPS = 8
F_GROUP_DIM = 128
D_FOURIER = F_GROUPS * F_GROUP_DIM
N_EXPERT_GROUPS = 8
TOPK_GROUPS = 4
TOP_K = 8
ROUTED_SCALE = 2.5
EPS = 1e-6
NEG_MASK = -1e30

LANES = 128
SUBLANES = 8
EXPERT_ROWS = 256
MIB = 1024 * 1024


def _cparams(sem, vmem_mib):
    return pltpu.CompilerParams(dimension_semantics=sem, vmem_limit_bytes=vmem_mib * MIB)


def _sigmoid(v):
    return 1.0 / (1.0 + jnp.exp(-v))


def _split_bf16(a):
    hi = a.astype(BF16)
    lo = (a - hi.astype(F32)).astype(BF16)
    return hi, lo


def _dot3(a, b, dims):
    ah, al = _split_bf16(a)
    bh, bl = _split_bf16(b)
    d = lambda p, q: lax.dot_general(p, q, dims, preferred_element_type=F32)
    return d(ah, bh) + (d(ah, bl) + d(al, bh))


_NN = (((1,), (0,)), ((), ()))
_NT = (((1,), (1,)), ((), ()))


def _ada_kernel(c_ref, w_ref, b_ref, o_ref):
    c = c_ref[...]
    a = c * _sigmoid(c)
    o_ref[...] = _dot3(a, w_ref[...], _NN) + b_ref[...]


def _ada(c8, w_ada, b_ada):
    d, n = w_ada.shape
    tn = 512
    return pl.pallas_call(
        _ada_kernel,
        out_shape=jax.ShapeDtypeStruct((8, n), F32),
        grid=(n // tn,),
        in_specs=[pl.BlockSpec((8, d), lambda j: (0, 0)),
                  pl.BlockSpec((d, tn), lambda j: (0, j)),
                  pl.BlockSpec((1, tn), lambda j: (0, j))],
        out_specs=pl.BlockSpec((8, tn), lambda j: (0, j)),
        compiler_params=_cparams(("parallel",), 48),
        name="ada",
    )(c8, w_ada, b_ada)


def _proj_kernel(x_ref, g_ref, sc_ref, sh_ref, w_ref, o_ref, h_ref):
    @pl.when(pl.program_id(2) == 0)
    def _():
        x = x_ref[0]
        ms = jnp.mean(x * x, axis=-1, keepdims=True)
        y = (x * lax.rsqrt(ms + EPS)) * g_ref[...]
        h_ref[...] = (y * (1.0 + sc_ref[0]) + sh_ref[0]).astype(BF16)

    o_ref[0] = jnp.dot(h_ref[...], w_ref[...], preferred_element_type=F32).astype(o_ref.dtype)


def _proj(x, g, sc, sh, w, col0, ncols, tm, tn):
    b, s, d = x.shape
    j0 = col0 // tn
    return pl.pallas_call(
        _proj_kernel,
        out_shape=jax.ShapeDtypeStruct((b, s, ncols), BF16),
        grid=(b, s // tm, ncols // tn),
        in_specs=[pl.BlockSpec((1, tm, d), lambda bi, i, j: (bi, i, 0)),
                  pl.BlockSpec((1, d), lambda bi, i, j: (0, 0)),
                  pl.BlockSpec((1, 1, d), lambda bi, i, j: (bi, 0, 0)),
                  pl.BlockSpec((1, 1, d), lambda bi, i, j: (bi, 0, 0)),
                  pl.BlockSpec((d, tn), lambda bi, i, j: (0, j + j0))],
        out_specs=pl.BlockSpec((1, tm, tn), lambda bi, i, j: (bi, i, j)),
        scratch_shapes=[pltpu.VMEM((tm, d), BF16)],
        compiler_params=_cparams(("parallel", "parallel", "arbitrary"), 48),
        name="proj",
    )(x, g, sc, sh, w)


NA_ROWS_PER_STEP = 8
NA_TOK = NA_ROWS_PER_STEP * GRID_W


def _na_bias_table(rpb):
    q = np.arange(GRID_W)
    kc = np.arange(GRID_W)
    ws = np.clip(q - NA_KW // 2, 0, GRID_W - NA_KW)
    inwin = (kc[None, :] >= ws[:, None]) & (kc[None, :] < ws[:, None] + NA_KW)
    ext = jnp.pad(rpb.astype(F32), ((0, 0), (0, 0), (GRID_W - NA_KW, GRID_W - NA_KW)), mode="edge")
    c = jnp.stack([ext[:, :, GRID_W - 1 - qi:2 * GRID_W - 1 - qi] for qi in range(GRID_W)], axis=2)
    c = jnp.where(jnp.asarray(inwin)[None, None], c, NEG_MASK)
    c2 = jnp.concatenate([c[:, :-1], c[:, 1:]], axis=-1)
    nd = c2.shape[1]
    c2 = c2.reshape(N_HEADS // 2, 2, nd, GRID_W, 2 * GRID_W).transpose(0, 2, 1, 3, 4)
    return c2.reshape(N_HEADS // 2, nd, 2 * GRID_W, 2 * GRID_W)


def _na_kernel(q_ref, kp_ref, kc_ref, kn_ref, vp_ref, vc_ref, vn_ref, kx_ref, vx_ref, bias_ref,
               o_ref, kbuf, vbuf, q_scr, s_scr, p_scr, *, rows):
    t = pl.program_id(2)
    kbuf[0:NA_TOK] = kp_ref[0]
    kbuf[NA_TOK:2 * NA_TOK] = kc_ref[0]
    kbuf[2 * NA_TOK:3 * NA_TOK] = kn_ref[0]
    vbuf[0:NA_TOK] = vp_ref[0]
    vbuf[NA_TOK:2 * NA_TOK] = vc_ref[0]
    vbuf[2 * NA_TOK:3 * NA_TOK] = vn_ref[0]
    even = lax.broadcasted_iota(jnp.int32, (GRID_W, LANES), 1) < HEAD_DIM
    kx = kx_ref[0]
    vx = vx_ref[0]
    nwin = NA_KH * GRID_W
    qrows = 2 * GRID_W

    offs = []
    for i in range(NA_ROWS_PER_STEP):
        q = q_ref[0, i * GRID_W:(i + 1) * GRID_W, :]
        zero = jnp.zeros_like(q)
        q_scr[i * qrows:i * qrows + GRID_W] = jnp.where(even, q, zero) * 0.125
        q_scr[i * qrows + GRID_W:(i + 1) * qrows] = jnp.where(even, zero, q) * 0.125
    s_scr[:, nwin:] = lax.dot_general(q_scr[...], kx, _NT, preferred_element_type=F32)
    for i in range(NA_ROWS_PER_STEP):
        r = t * NA_ROWS_PER_STEP + i
        rs = jnp.clip(r - NA_KH // 2, 0, rows - NA_KH)
        off = pl.multiple_of((rs - (t - 1) * NA_ROWS_PER_STEP) * GRID_W, GRID_W)
        offs.append(off)
        d0 = rs - r + NA_KH - 1
        s_lat = lax.dot_general(q_scr[i * qrows:(i + 1) * qrows], kbuf[pl.ds(off, nwin), :], _NT,
                                preferred_element_type=F32)
        bias = jnp.concatenate([bias_ref[0, d0 + 2 * j] for j in range(NA_KH // 2)], axis=1)
        s_scr[i * qrows:(i + 1) * qrows, 0:nwin] = s_lat + bias
    inv_l = []
    for i in range(NA_ROWS_PER_STEP):
        s = s_scr[i * qrows:(i + 1) * qrows, :]
        p = jnp.exp(s - jnp.max(s, axis=1, keepdims=True))
        inv_l.append(1.0 / jnp.sum(p, axis=1, keepdims=True))
        p_scr[i * qrows:(i + 1) * qrows, :] = p.astype(BF16)
    o_ctx = jnp.dot(p_scr[:, nwin:], vx, preferred_element_type=F32)
    for i in range(NA_ROWS_PER_STEP):
        o2 = jnp.dot(p_scr[i * qrows:(i + 1) * qrows, 0:nwin], vbuf[pl.ds(offs[i], nwin), :],
                     preferred_element_type=F32)
        o2 = (o2 + o_ctx[i * qrows:(i + 1) * qrows]) * inv_l[i]
        o = jnp.where(even, o2[0:GRID_W], o2[GRID_W:qrows])
        o_ref[0, i * GRID_W:(i + 1) * GRID_W, :] = o.astype(o_ref.dtype)


def _na_attention(proj, kvx, bias):
    b, s, _ = proj.shape
    lctx = kvx.shape[1]
    rows = s // GRID_W
    ngrp = rows // NA_ROWS_PER_STEP
    npair = N_HEADS // 2
    kcol, vcol = D_NA // LANES, 2 * D_NA // LANES
    blk = (1, NA_TOK, LANES)
    prev = lambda t: jnp.maximum(t - 1, 0)
    nxt = lambda t: jnp.minimum(t + 1, ngrp - 1)
    return pl.pallas_call(
        functools.partial(_na_kernel, rows=rows),
        out_shape=jax.ShapeDtypeStruct((b, s, D_NA), BF16),
        grid=(b, npair, ngrp),
        in_specs=[pl.BlockSpec(blk, lambda bi, p, t: (bi, t, p)),
                  pl.BlockSpec(blk, lambda bi, p, t: (bi, prev(t), kcol + p)),
                  pl.BlockSpec(blk, lambda bi, p, t: (bi, t, kcol + p)),
                  pl.BlockSpec(blk, lambda bi, p, t: (bi, nxt(t), kcol + p)),
                  pl.BlockSpec(blk, lambda bi, p, t: (bi, prev(t), vcol + p)),
                  pl.BlockSpec(blk, lambda bi, p, t: (bi, t, vcol + p)),
                  pl.BlockSpec(blk, lambda bi, p, t: (bi, nxt(t), vcol + p)),
                  pl.BlockSpec((1, lctx, LANES), lambda bi, p, t: (bi, 0, p)),
                  pl.BlockSpec((1, lctx, LANES), lambda bi, p, t: (bi, 0, kcol + p)),
                  pl.BlockSpec((1,) + bias.shape[1:], lambda bi, p, t: (p, 0, 0, 0))],
        out_specs=pl.BlockSpec(blk, lambda bi, p, t: (bi, t, p)),
        scratch_shapes=[pltpu.VMEM((3 * NA_TOK, LANES), BF16), pltpu.VMEM((3 * NA_TOK, LANES), BF16),
                        pltpu.VMEM((2 * NA_TOK, LANES), BF16),
                        pltpu.VMEM((2 * NA_TOK, NA_KH * GRID_W + lctx), F32),
                        pltpu.VMEM((2 * NA_TOK, NA_KH * GRID_W + lctx), BF16)],
        compiler_params=_cparams(("parallel", "parallel", "arbitrary"), 32),
        name="na",
    )(proj, proj, proj, proj, proj, proj, proj, kvx, kvx, bias)


def _dft_consts(n1, n2):
    n = n1 * n2
    a1 = 2.0 * np.pi * np.outer(np.arange(n1), np.arange(n1)) / n1
    fa = np.concatenate([np.cos(a1), -np.sin(a1)], axis=0)
    tw = 2.0 * np.pi * np.outer(np.arange(n2), np.arange(n1)) / n
    twc = np.repeat(np.cos(tw)[:, :, None], LANES, axis=2)
    tws = np.repeat(np.sin(tw)[:, :, None], LANES, axis=2)
    a2 = 2.0 * np.pi * np.outer(np.arange(n2), np.arange(n2)) / n2
    c2, s2 = np.cos(a2), np.sin(a2)
    fb = np.block([[c2, s2], [-s2, c2]])
    ac = 2.0 * np.pi * np.outer(np.arange(F_GROUP_DIM), np.arange(F_GROUP_DIM)) / F_GROUP_DIM
    cs = np.concatenate([np.cos(ac), np.sin(ac)], axis=0) / np.sqrt(n * F_GROUP_DIM)
    f = lambda v: jnp.asarray(v.astype(np.float32))
    return f(fa), f(twc), f(tws), f(fb), f(cs)


DFT_STEP = 4


def _dft_a_kernel(x_ref, fa_ref, c_ref, s_ref, o_ref, *, n1):
    fa = fa_ref[...].astype(BF16)
    for j in range(DFT_STEP):
        y = jnp.dot(fa, x_ref[0, j], preferred_element_type=F32)
        c = c_ref[j]
        s = s_ref[j]
        for g in range(D_FOURIER // LANES):
            sl = slice(g * LANES, (g + 1) * LANES)
            yr, yi = y[0:n1, sl], y[n1:2 * n1, sl]
            o_ref[0, j, 0:n1, sl] = (yr * c + yi * s).astype(o_ref.dtype)
            o_ref[0, j, n1:2 * n1, sl] = (yi * c - yr * s).astype(o_ref.dtype)


def _dft_b_kernel(y_ref, fb_ref, cs_ref, o_ref, *, n2):
    fb = fb_ref[...].astype(BF16)
    cc = cs_ref[0:F_GROUP_DIM, :].astype(BF16)
    sc = cs_ref[F_GROUP_DIM:2 * F_GROUP_DIM, :].astype(BF16)
    for j in range(DFT_STEP):
        v = jnp.dot(fb, y_ref[0, j], preferred_element_type=F32)
        for g in range(F_GROUPS):
            sl = slice(g * F_GROUP_DIM, (g + 1) * F_GROUP_DIM)
            vr = v[0:n2, sl].astype(BF16)
            vi = v[n2:2 * n2, sl].astype(BF16)
            o_ref[0, j, :, sl] = (jnp.dot(vr, cc, preferred_element_type=F32)
                                  + jnp.dot(vi, sc, preferred_element_type=F32)).astype(o_ref.dtype)


def _fourier(u):
    b, s, c = u.shape
    n2 = GRID_W
    n1 = s // n2
    fa, twc, tws, fb, cs = _dft_consts(n1, n2)
    xt = u.reshape(b, n1, n2, c).transpose(0, 2, 1, 3)
    ya = pl.pallas_call(
        functools.partial(_dft_a_kernel, n1=n1),
        out_shape=jax.ShapeDtypeStruct((b, n2, 2 * n1, c), BF16),
        grid=(b, n2 // DFT_STEP),
        in_specs=[pl.BlockSpec((1, DFT_STEP, n1, c), lambda bi, j: (bi, j, 0, 0)),
                  pl.BlockSpec((2 * n1, n1), lambda bi, j: (0, 0)),
                  pl.BlockSpec((DFT_STEP, n1, LANES), lambda bi, j: (j, 0, 0)),
                  pl.BlockSpec((DFT_STEP, n1, LANES), lambda bi, j: (j, 0, 0))],
        out_specs=pl.BlockSpec((1, DFT_STEP, 2 * n1, c), lambda bi, j: (bi, j, 0, 0)),
        compiler_params=_cparams(("parallel", "parallel"), 32),
        name="dft_a",
    )(xt, fa, twc, tws)
    yt = ya.reshape(b, n2, 2, n1, c).transpose(0, 3, 2, 1, 4).reshape(b, n1, 2 * n2, c)
    vb = pl.pallas_call(
        functools.partial(_dft_b_kernel, n2=n2),
        out_shape=jax.ShapeDtypeStruct((b, n1, n2, c), BF16),
        grid=(b, n1 // DFT_STEP),
        in_specs=[pl.BlockSpec((1, DFT_STEP, 2 * n2, c), lambda bi, j: (bi, j, 0, 0)),
                  pl.BlockSpec((2 * n2, 2 * n2), lambda bi, j: (0, 0)),
                  pl.BlockSpec((2 * F_GROUP_DIM, F_GROUP_DIM), lambda bi, j: (0, 0))],
        out_specs=pl.BlockSpec((1, DFT_STEP, n2, c), lambda bi, j: (bi, j, 0, 0)),
        compiler_params=_cparams(("parallel", "parallel"), 32),
        name="dft_b",
    )(yt, fb, cs)
    return vb.transpose(0, 2, 1, 3).reshape(b, s, c)


def _merge_kernel(o_ref, f_ref, ga_ref, gb_ref, x_ref, g1_ref, sc_ref, sh_ref, gf_ref,
                  wna_ref, wf_ref, wo_ref, x1_ref, h2_ref):
    ya = jnp.dot(o_ref[0], wna_ref[...], preferred_element_type=F32)
    yf = jnp.dot(f_ref[0], wf_ref[...], preferred_element_type=F32)
    m = _sigmoid(ga_ref[0].astype(F32)) * ya + _sigmoid(gb_ref[0].astype(F32)) * yf
    out = jnp.dot(m.astype(BF16), wo_ref[...], preferred_element_type=F32)
    x1 = x_ref[0] + g1_ref[0] * out
    x1_ref[0] = x1
    ms = jnp.mean(x1 * x1, axis=-1, keepdims=True)
    y = (x1 * lax.rsqrt(ms + EPS)) * gf_ref[...]
    h2_ref[0] = y * (1.0 + sc_ref[0]) + sh_ref[0]


def _merge(o_na, four, proj, x, ga1, sc2, sh2, g_ffn, w_na, w_four, w_out):
    b, s, d = x.shape
    tm = 256
    ga_blk = (D_NA * 3 + D_FOURIER) // d
    row = lambda bi, i: (bi, i, 0)
    per_b = lambda bi, i: (bi, 0, 0)
    full = lambda bi, i: (0, 0)
    return pl.pallas_call(
        _merge_kernel,
        out_shape=(jax.ShapeDtypeStruct((b, s, d), F32), jax.ShapeDtypeStruct((b, s, d), F32)),
        grid=(b, s // tm),
        in_specs=[pl.BlockSpec((1, tm, D_NA), row),
                  pl.BlockSpec((1, tm, D_FOURIER), row),
                  pl.BlockSpec((1, tm, d), lambda bi, i: (bi, i, ga_blk)),
                  pl.BlockSpec((1, tm, d), lambda bi, i: (bi, i, ga_blk + 1)),
                  pl.BlockSpec((1, tm, d), row),
                  pl.BlockSpec((1, 1, d), per_b),
                  pl.BlockSpec((1, 1, d), per_b),
                  pl.BlockSpec((1, 1, d), per_b),
                  pl.BlockSpec((1, d), full),
                  pl.BlockSpec((D_NA, d), full),
                  pl.BlockSpec((D_FOURIER, d), full),
                  pl.BlockSpec((d, d), full)],
        out_specs=(pl.BlockSpec((1, tm, d), row), pl.BlockSpec((1, tm, d), row)),
        compiler_params=_cparams(("parallel", "parallel"), 56),
        name="merge",
    )(o_na, four, proj, proj, x, ga1, sc2, sh2, g_ffn, w_na, w_four, w_out)


def _router_kernel(h_ref, wt_ref, b_ref, idx_ref, w_ref, rank_ref, cnt_ref, base_scr, *, n_exp):
    tm = h_ref.shape[0]

    @pl.when(pl.program_id(0) == 0)
    def _():
        base_scr[...] = jnp.zeros_like(base_scr)

    gsz = n_exp // N_EXPERT_GROUPS
    s = _sigmoid(_dot3(wt_ref[...], h_ref[...], _NT))
    ssel = s + b_ref[...]
    rowf = lax.broadcasted_iota(jnp.int32, (n_exp, tm), 0).astype(F32)
    big = float(n_exp)
    neg = -jnp.inf
    cmax = lambda v: jnp.max(v, axis=0, keepdims=True)
    cmin = lambda v: jnp.min(v, axis=0, keepdims=True)

    gs = []
    for g in range(N_EXPERT_GROUPS):
        xg = ssel[g * gsz:(g + 1) * gsz]
        ig = (lax.broadcasted_iota(jnp.int32, (gsz, tm), 0) + g * gsz).astype(F32)
        m1 = cmax(xg)
        i1 = cmin(jnp.where(xg == m1, ig, big))
        m2 = cmax(jnp.where(ig == i1, neg, xg))
        gs.append(m1 + m2)
    parts = []
    for g in range(N_EXPERT_GROUPS):
        beaten = jnp.zeros((1, tm), F32)
        for o in range(N_EXPERT_GROUPS):
            if o != g:
                wins = (gs[o] >= gs[g]) if o < g else (gs[o] > gs[g])
                beaten = beaten + jnp.where(wins, 1.0, 0.0)
        keep = beaten < float(TOPK_GROUPS)
        parts.append(jnp.where(keep, ssel[g * gsz:(g + 1) * gsz], neg))
    cand = jnp.concatenate(parts, axis=0)
    ws, picks = [], []
    for k in range(TOP_K):
        m = cmax(cand)
        i = cmin(jnp.where(cand == m, rowf, big))
        hit = rowf == i
        ws.append(jnp.sum(jnp.where(hit, s, 0.0), axis=0, keepdims=True))
        picks.append(i)
        idx_ref[k:k + 1, :] = i.astype(jnp.int32)
        cand = jnp.where(hit, neg, cand)
    den = ws[0]
    for k in range(1, TOP_K):
        den = den + ws[k]
    for k in range(TOP_K):
        w_ref[k:k + 1, :] = ws[k] / den * ROUTED_SCALE

    hits = jnp.zeros((n_exp, tm), F32)
    for k in range(TOP_K):
        hits = hits + jnp.where(rowf == picks[k], 1.0, 0.0)
    earlier = (lax.broadcasted_iota(jnp.int32, (tm, tm), 0) < lax.broadcasted_iota(jnp.int32, (tm, tm), 1))
    before = jnp.dot(hits.astype(BF16), jnp.where(earlier, 1.0, 0.0).astype(BF16),
                     preferred_element_type=F32) + base_scr[...]
    for k in range(TOP_K):
        sel = rowf == picks[k]
        rank_ref[k:k + 1, :] = jnp.sum(jnp.where(sel, before, 0.0), axis=0, keepdims=True).astype(jnp.int32)
    base_scr[...] = base_scr[...] + jnp.sum(hits, axis=1, keepdims=True)
    cnt_ref[...] = base_scr[...]


def _router(h2, w_router_t, b_router):
    n, d = h2.shape
    n_exp = w_router_t.shape[0]
    tm = 512
    tok = lambda i: (0, i)
    return pl.pallas_call(
        functools.partial(_router_kernel, n_exp=n_exp),
        out_shape=(jax.ShapeDtypeStruct((TOP_K, n), jnp.int32), jax.ShapeDtypeStruct((TOP_K, n), F32),
                   jax.ShapeDtypeStruct((TOP_K, n), jnp.int32), jax.ShapeDtypeStruct((n_exp, 1), F32)),
        grid=(n // tm,),
        in_specs=[pl.BlockSpec((tm, d), lambda i: (i, 0)),
                  pl.BlockSpec((n_exp, d), lambda i: (0, 0)),
                  pl.BlockSpec((n_exp, 1), lambda i: (0, 0))],
        out_specs=(pl.BlockSpec((TOP_K, tm), tok), pl.BlockSpec((TOP_K, tm), tok), pl.BlockSpec((TOP_K, tm), tok),
                   pl.BlockSpec((n_exp, 1), lambda i: (0, 0))),
        scratch_shapes=[pltpu.VMEM((n_exp, 1), F32)],
        compiler_params=_cparams(("arbitrary",), 32),
        name="router",
    )(h2, w_router_t, b_router)


def _dest_kernel(idx_ref, rank_ref, ps_ref, o_ref, *, n_exp):
    tm = idx_ref.shape[1]
    rowf = lax.broadcasted_iota(jnp.int32, (n_exp, tm), 0).astype(F32)
    ps = ps_ref[...]
    for k in range(TOP_K):
        sel = rowf == idx_ref[k:k + 1, :].astype(F32)
        dest = jnp.sum(jnp.where(sel, ps, 0.0), axis=0, keepdims=True).astype(jnp.int32) + rank_ref[k:k + 1, :]
        for j in range(tm // LANES):
            o_ref[j, k:k + 1, :] = dest[:, j * LANES:(j + 1) * LANES]


def _dest(idx_kn, rank_kn, pstart_f):
    k, n = idx_kn.shape
    n_exp = pstart_f.shape[0]
    tm = 2048
    return pl.pallas_call(
        functools.partial(_dest_kernel, n_exp=n_exp),
        out_shape=jax.ShapeDtypeStruct((n // LANES, k, LANES), jnp.int32),
        grid=(n // tm,),
        in_specs=[pl.BlockSpec((k, tm), lambda i: (0, i)),
                  pl.BlockSpec((k, tm), lambda i: (0, i)),
                  pl.BlockSpec((n_exp, 1), lambda i: (0, 0))],
        out_specs=pl.BlockSpec((tm // LANES, k, LANES), lambda i: (i, 0, 0)),
        compiler_params=_cparams(("parallel",), 32),
        name="dest",
    )(idx_kn, rank_kn, pstart_f)


TOKEN_TILE = LANES


def _to_row_tiles(x):
    return x.reshape(x.shape[0], x.shape[1] // LANES, LANES)


def _from_row_tiles(x):
    return x.reshape(x.shape[0], x.shape[1] * x.shape[2])


def _fill_unowned_rows(pad_start_ref, pad_len_ref, nu_ref, zeros, xs_hbm, zsem, tb, wait):
    n_exp = pad_start_ref.shape[0]
    nb = xs_hbm.shape[0] // tb

    def go(row, nrows):
        c = pltpu.make_async_copy(zeros.at[pl.ds(0, nrows)], xs_hbm.at[pl.ds(row, nrows)], zsem)
        c.wait() if wait else c.start()

    def per_expert(e, carry):
        row = pad_start_ref[e]
        left = pad_len_ref[e]
        singles = left & (SUBLANES - 1)
        for r in range(SUBLANES - 1):
            @pl.when(r < singles)
            def _(r=r):
                go(row + r, 1)
        row = row + singles
        chunk = SUBLANES
        while chunk < tb:
            @pl.when((left & chunk) != 0)
            def _(row=row, chunk=chunk):
                go(pl.multiple_of(row, chunk), chunk)
            row = row + (left & chunk)
            chunk *= 2
        return carry

    lax.fori_loop(0, n_exp, per_expert, 0)

    def per_tail_block(b, carry):
        go(pl.multiple_of(b * tb, tb), tb)
        return carry

    lax.fori_loop(nu_ref[0], nb, per_tail_block, 0)


def _dispatch_kernel(pad_start_ref, pad_len_ref, nu_ref, dest_ref, h_ref, xs_hbm, stage, zeros, sem, zsem, *, tb):
    j = pl.program_id(0)
    nj = pl.num_programs(0)
    slot = j % 2
    tm = h_ref.shape[0]

    def wait_slot(s):
        for _ in range(TOP_K):
            pltpu.make_async_copy(stage.at[s], xs_hbm.at[pl.ds(0, tm)], sem.at[s]).wait()

    @pl.when(j == 0)
    def _():
        zeros[...] = jnp.zeros_like(zeros)
        _fill_unowned_rows(pad_start_ref, pad_len_ref, nu_ref, zeros, xs_hbm, zsem, tb, wait=False)

    packed = _to_row_tiles(h_ref[...].astype(BF16))
    for s in range(2):
        @pl.when(slot == s)
        def _(s=s):
            @pl.when(j >= 2)
            def _():
                wait_slot(s)

            stage[s] = packed
            for i in range(tm):
                for k in range(TOP_K):
                    pltpu.make_async_copy(stage.at[s, i], xs_hbm.at[dest_ref[0, k, i]],
                                          sem.at[s]).start(priority=k % 2)

    @pl.when(j == nj - 1)
    def _():
        wait_slot(slot)

    @pl.when(jnp.logical_and(j == nj - 1, nj >= 2))
    def _():
        wait_slot(1 - slot)

    @pl.when(j == nj - 1)
    def _():
        _fill_unowned_rows(pad_start_ref, pad_len_ref, nu_ref, zeros, xs_hbm, zsem, tb, wait=True)


def _dispatch(dest3, h2, pad_start, pad_len, n_used, n_rows, tb):
    n, d = h2.shape
    tm = TOKEN_TILE
    grid_spec = pltpu.PrefetchScalarGridSpec(
        num_scalar_prefetch=3,
        grid=(n // tm,),
        in_specs=[pl.BlockSpec((1, TOP_K, tm), lambda j, ps, pn, nu: (j, 0, 0), memory_space=pltpu.SMEM),
                  pl.BlockSpec((tm, d), lambda j, ps, pn, nu: (j, 0))],
        out_specs=pl.BlockSpec(memory_space=pl.ANY),
        scratch_shapes=[pltpu.VMEM((2, tm, d // LANES, LANES), BF16), pltpu.VMEM((tb, d // LANES, LANES), BF16),
                        pltpu.SemaphoreType.DMA((2,)), pltpu.SemaphoreType.DMA(())],
    )
    return pl.pallas_call(
        functools.partial(_dispatch_kernel, tb=tb),
        out_shape=jax.ShapeDtypeStruct((n_rows, d // LANES, LANES), BF16),
        grid_spec=grid_spec,
        compiler_params=_cparams(("arbitrary",), 32),
        name="dispatch",
    )(pad_start, pad_len, n_used, dest3, h2)


WEIGHT_SLOTS = 3


def _expert_kernel(be_ref, first_ref, slot_ref, ahead_ref, prime_ref, nu_ref,
                   x_ref, wg_hbm, wu_hbm, wd_hbm, o_ref, wg_buf, wu_buf, wd_buf, sem):
    b = pl.program_id(0)
    nu = nu_ref[0]
    slot = slot_ref[b]

    def weight_copies(e, s):
        return (pltpu.make_async_copy(wg_hbm.at[e], wg_buf.at[s], sem.at[s, 0]),
                pltpu.make_async_copy(wu_hbm.at[e], wu_buf.at[s], sem.at[s, 1]),
                pltpu.make_async_copy(wd_hbm.at[e], wd_buf.at[s], sem.at[s, 2]))

    def start_weights(e, s):
        for i, c in enumerate(weight_copies(e, s)):
            c.start(priority=min(i, 1))

    @pl.when(b == 0)
    def _():
        for s in range(WEIGHT_SLOTS - 1):
            @pl.when(prime_ref[s] >= 0)
            def _(s=s):
                start_weights(prime_ref[s], s)

    @pl.when(first_ref[b] == 1)
    def _():
        for c in weight_copies(be_ref[b], slot):
            c.wait()
        nxt = ahead_ref[b]

        @pl.when(nxt >= 0)
        def _():
            start_weights(nxt, (slot + WEIGHT_SLOTS - 1) % WEIGHT_SLOTS)

    @pl.when(b < nu)
    def _():
        x = _from_row_tiles(x_ref[...])
        g = jnp.dot(x, wg_buf[slot].astype(BF16), preferred_element_type=F32)
        u = jnp.dot(x, wu_buf[slot].astype(BF16), preferred_element_type=F32)
        a = (g * _sigmoid(g)) * u
        y = jnp.dot(a.astype(BF16), wd_buf[slot].astype(BF16), preferred_element_type=F32)
        o_ref[...] = _to_row_tiles(y.astype(BF16))

    @pl.when(b >= nu)
    def _():
        o_ref[...] = jnp.zeros_like(o_ref)


def _experts(xs, blocks, w_eg, w_eu, w_ed, tb):
    p = xs.shape[0]
    blk = (tb,) + xs.shape[1:]
    nb = p // tb
    n_exp, d, f = w_eg.shape
    last_used = lambda b, nu: jnp.minimum(b, nu[0] - 1)
    grid_spec = pltpu.PrefetchScalarGridSpec(
        num_scalar_prefetch=6,
        grid=(nb,),
        in_specs=[pl.BlockSpec(blk, lambda b, be, fi, sl, ah, pr, nu: (last_used(b, nu), 0, 0)),
                  pl.BlockSpec(memory_space=pl.ANY),
                  pl.BlockSpec(memory_space=pl.ANY),
                  pl.BlockSpec(memory_space=pl.ANY)],
        out_specs=pl.BlockSpec(blk, lambda b, be, fi, sl, ah, pr, nu: (b, 0, 0)),
        scratch_shapes=[pltpu.VMEM((WEIGHT_SLOTS, d, f), F32), pltpu.VMEM((WEIGHT_SLOTS, d, f), F32),
                        pltpu.VMEM((WEIGHT_SLOTS, f, d), F32), pltpu.SemaphoreType.DMA((WEIGHT_SLOTS, 3))],
    )
    return pl.pallas_call(
        _expert_kernel,
        out_shape=jax.ShapeDtypeStruct(xs.shape, BF16),
        grid_spec=grid_spec,
        compiler_params=_cparams(("arbitrary",), 58),
        name="experts",
    )(*blocks, xs, w_eg, w_eu, w_ed)


def _shared_kernel(h_ref, wg_ref, wu_ref, wd_ref, o_ref):
    x = h_ref[...].astype(BF16)
    g = jnp.dot(x, wg_ref[...], preferred_element_type=F32)
    u = jnp.dot(x, wu_ref[...], preferred_element_type=F32)
    a = (g * _sigmoid(g)) * u
    o_ref[...] = jnp.dot(a.astype(BF16), wd_ref[...], preferred_element_type=F32)


def _shared(h2, w_sg, w_su, w_sd):
    n, d = h2.shape
    f = w_sg.shape[1]
    tm = 512
    return pl.pallas_call(
        _shared_kernel,
        out_shape=jax.ShapeDtypeStruct((n, d), F32),
        grid=(n // tm,),
        in_specs=[pl.BlockSpec((tm, d), lambda i: (i, 0)),
                  pl.BlockSpec((d, f), lambda i: (0, 0)),
                  pl.BlockSpec((d, f), lambda i: (0, 0)),
                  pl.BlockSpec((f, d), lambda i: (0, 0))],
        out_specs=pl.BlockSpec((tm, d), lambda i: (i, 0)),
        compiler_params=_cparams(("parallel",), 40),
        name="shared",
    )(h2, w_sg, w_su, w_sd)


def _gather_sorted_rows(dest_ref, ys_hbm, dst, sem):
    tm = dest_ref.shape[2]
    for k in range(TOP_K):
        for i in range(tm):
            pltpu.make_async_copy(ys_hbm.at[dest_ref[0, k, i]], dst.at[k * tm + i],
                                  sem).start(priority=i % 2)


def _combine_kernel(dstc_ref, dstn_ref, w_ref, ys_hbm, sh_ref, x1_ref, g2_ref, gf_ref, o_ref, buf, acc_scr, sem):
    j = pl.program_id(0)
    nj = pl.num_programs(0)
    slot = j % 2
    tm, d = o_ref.shape

    @pl.when(j == 0)
    def _():
        _gather_sorted_rows(dstc_ref, ys_hbm, buf.at[0], sem.at[0])

    for s in range(2):
        @pl.when(jnp.logical_and(j + 1 < nj, slot == 1 - s))
        def _(s=s):
            _gather_sorted_rows(dstn_ref, ys_hbm, buf.at[s], sem.at[s])

    pltpu.make_async_copy(ys_hbm.at[pl.ds(0, TOP_K * tm)], buf.at[slot], sem.at[slot]).wait()
    for i in range(tm):
        acc = w_ref[0, 0, i] * buf[slot, i].astype(F32)
        for k in range(1, TOP_K):
            acc = acc + w_ref[0, k, i] * buf[slot, k * tm + i].astype(F32)
        acc_scr[i] = acc
    gf = gf_ref[...]
    g2 = g2_ref[0]
    rows = 32
    for r in range(0, tm, rows):
        acc = _from_row_tiles(acc_scr[r:r + rows])
        x2 = x1_ref[r:r + rows, :] + g2 * (acc + sh_ref[r:r + rows, :])
        ms = jnp.mean(x2 * x2, axis=-1, keepdims=True)
        o_ref[r:r + rows, :] = (x2 * lax.rsqrt(ms + EPS)) * gf


def _combine(dest3, w3, ys, shared, x1, ga2, g_final, seq):
    n, d = x1.shape
    tm = TOKEN_TILE
    nj = n // tm
    smem = lambda f: pl.BlockSpec((1, TOP_K, tm), f, memory_space=pltpu.SMEM)
    return pl.pallas_call(
        _combine_kernel,
        out_shape=jax.ShapeDtypeStruct((n, d), F32),
        grid=(nj,),
        in_specs=[smem(lambda j: (j, 0, 0)),
                  smem(lambda j: (jnp.minimum(j + 1, nj - 1), 0, 0)),
                  smem(lambda j: (j, 0, 0)),
                  pl.BlockSpec(memory_space=pl.ANY),
                  pl.BlockSpec((tm, d), lambda j: (j, 0)),
                  pl.BlockSpec((tm, d), lambda j: (j, 0)),
                  pl.BlockSpec((1, 1, d), lambda j: ((j * tm) // seq, 0, 0)),
                  pl.BlockSpec((1, d), lambda j: (0, 0))],
        out_specs=pl.BlockSpec((tm, d), lambda j: (j, 0)),
        scratch_shapes=[pltpu.VMEM((2, TOP_K * tm) + ys.shape[1:], BF16), pltpu.VMEM((tm,) + ys.shape[1:], F32),
                        pltpu.SemaphoreType.DMA((2,))],
        compiler_params=_cparams(("arbitrary",), 40),
        name="combine",
    )(dest3, dest3, w3, ys, shared, x1, ga2, g_final)


def _block_tables(counts, tb, nb):
    n_exp = counts.shape[0]
    padded = (counts + tb - 1) // tb * tb
    pend = jnp.cumsum(padded)
    pstart = pend - padded
    blk_row = jnp.arange(nb, dtype=jnp.int32) * tb
    be = jnp.minimum(jnp.sum((pend[None, :] <= blk_row[:, None]).astype(jnp.int32), axis=1), n_exp - 1)
    n_used = pend[-1] // tb
    nonempty = counts > 0
    ordinal = jnp.cumsum(nonempty.astype(jnp.int32)) - 1
    by_ord = jnp.sum(jnp.where((ordinal[None, :] == jnp.arange(n_exp + WEIGHT_SLOTS)[:, None]) & nonempty[None, :],
                               jnp.arange(n_exp, dtype=jnp.int32)[None, :] - n_exp, 0), axis=1) + n_exp
    sel = (be[:, None] == jnp.arange(n_exp, dtype=jnp.int32)[None, :]).astype(jnp.int32)
    pick = lambda v: jnp.sum(sel * v[None, :], axis=1)
    b_start, b_ord = pick(pstart), pick(ordinal)
    ord_sel = (b_ord[:, None] + (WEIGHT_SLOTS - 1) == jnp.arange(n_exp + WEIGHT_SLOTS)[None, :]).astype(jnp.int32)
    b_ahead = jnp.sum(ord_sel * by_ord[None, :], axis=1)
    used = jnp.arange(nb) < n_used
    first = jnp.logical_and(blk_row == b_start, used).astype(jnp.int32)
    b_ahead = jnp.where(b_ahead < n_exp, b_ahead, -1)
    prime = jnp.where(by_ord[:WEIGHT_SLOTS - 1] < n_exp, by_ord[:WEIGHT_SLOTS - 1], -1)
    i32 = lambda v: v.astype(jnp.int32)
    blocks = (i32(be), first, i32(b_ord % WEIGHT_SLOTS), i32(b_ahead), i32(prime), i32(n_used).reshape(1))
    pads = (i32(pstart + counts), i32(padded - counts), i32(n_used).reshape(1))
    return pstart, blocks, pads


def kernel(x, c, ctx, c_ctx, w_ada, b_ada, g_mix, w_in, na_rpb, w_na, w_four, w_out, g_ffn, w_router, b_router,
           w_exp_gate, w_exp_up, w_exp_down, w_sh_gate, w_sh_up, w_sh_down, g_final):
    depth = w_ada.shape[0]
    assert depth == 1, "single-layer kernel: the context stream is never updated"
    b, s, d = x.shape
    n = b * s
    n_exp = w_router.shape[-1]
    rows = s // GRID_W
    assert s % (GRID_W * NA_ROWS_PER_STEP) == 0 and rows >= NA_KH and c.shape[0] + 1 <= 8

    c8 = jnp.zeros((8, d), F32).at[:b].set(c).at[b].set(c_ctx)
    mod = _ada(c8, w_ada[0], b_ada[0][None])
    sh1, sc1, ga1, sh2, sc2, ga2 = [m[:b, None, :] for m in jnp.split(mod, 6, axis=-1)]
    csh1, csc1 = [jnp.broadcast_to(m[b][None, None, :], (b, 1, d)) for m in jnp.split(mod, 6, axis=-1)[:2]]

    w_in_b = w_in[0].astype(BF16)
    g_mix2 = g_mix[0][None]
    proj = _proj(x, g_mix2, sc1, sh1, w_in_b, 0, w_in_b.shape[1], 1024, 1024)
    kvx = _proj(ctx, g_mix2, csc1, csh1, w_in_b, D_NA, 2 * D_NA, ctx.shape[1], 512)

    o_na = _na_attention(proj, kvx, _na_bias_table(na_rpb[0]))
    four = _fourier(proj[:, :, 3 * D_NA:3 * D_NA + D_FOURIER])

    x1, h2 = _merge(o_na, four, proj, x, ga1, sc2, sh2, g_ffn[0][None],
                    w_na[0].astype(BF16), w_four[0].astype(BF16), w_out[0].astype(BF16))
    x1 = x1.reshape(n, d)
    h2 = h2.reshape(n, d)

    idx_kn, w_kn, rank_kn, counts = _router(h2, w_router[0].T, b_router[0][:, None])
    nb = (n * TOP_K + n_exp * (EXPERT_ROWS - 1)) // EXPERT_ROWS
    pstart, blocks, pads = _block_tables(counts[:, 0].astype(jnp.int32), EXPERT_ROWS, nb)
    dest3 = _dest(idx_kn, rank_kn, pstart.astype(F32)[:, None])
    xs = _dispatch(dest3, h2, *pads, nb * EXPERT_ROWS, EXPERT_ROWS)
    ys = _experts(xs, blocks, w_exp_gate[0], w_exp_up[0], w_exp_down[0], EXPERT_ROWS)
    shared = _shared(h2, w_sh_gate[0].astype(BF16), w_sh_up[0].astype(BF16), w_sh_down[0].astype(BF16))
    w3 = w_kn.reshape(TOP_K, n // TOKEN_TILE, TOKEN_TILE).transpose(1, 0, 2)
    out = _combine(dest3, w3, ys, shared, x1, ga2, g_final[None], s)
    return out.reshape(b, s, d)
```
